```python
import jax, jax.numpy as jnp
from jax import lax
import numpy as np

D_MODEL = 1024
BATCH = 8
SEQ = 8192
DEPTH = 1
DEC_BATCH = 128
DEC_SEQ = 8
PAST_LEN = 8192
PAGE_SIZE = 128

FOX_HEADS = 8
FOX_HEAD_DIM = 64
FOX_WIDTH = FOX_HEADS * FOX_HEAD_DIM
CONV_WIDTH = D_MODEL // 2
CONV_K = 3
Q_BLOCK = 128
MEM_LEN = 256
XA_HEADS = 4
XA_HEAD_DIM = D_MODEL // XA_HEADS
PEER_HEADS = 8
PEER_N_KEYS = 128
PEER_N_EXPERTS = PEER_N_KEYS * PEER_N_KEYS
PEER_KEY_DIM = 128
PEER_HALF = PEER_KEY_DIM // 2
PEER_TOPK = 16
PEER_BLOCK = 256
RMS_EPS = 1e-6
FORGET_BIAS = 4.0

SPLIT_SIZES = (FOX_WIDTH, FOX_WIDTH, FOX_WIDTH, FOX_HEADS, CONV_WIDTH, CONV_WIDTH, CONV_WIDTH, D_MODEL, D_MODEL)
SPLIT_IDX = tuple(int(i) for i in np.cumsum(SPLIT_SIZES)[:-1])
IN_PROJ_WIDTH = sum(SPLIT_SIZES)

kernel_name = 'fox_shortconv_peer_hybrid_step'


def rmsnorm(x, g):
    xf = x.astype(jnp.float32)
    y = xf * lax.rsqrt(jnp.mean(xf * xf, axis=-1, keepdims=True) + RMS_EPS)
    return (y * g.astype(jnp.float32)).astype(x.dtype)


def mix_inputs(xn, w_in, b_forget):
    B, T, _ = xn.shape
    z = xn @ w_in
    q, k, v, fl, ch, cc, cb, ga, gb = jnp.split(z, SPLIT_IDX, axis=-1)
    heads = lambda a: a.reshape(B, T, FOX_HEADS, FOX_HEAD_DIM)
    logf = jax.nn.log_sigmoid(fl.astype(jnp.float32) + b_forget.astype(jnp.float32))
    return heads(q), heads(k), heads(v), logf, cc * ch, cb, ga, gb


def fox_prompt(q, k, v, logf):
    B, T = q.shape[:2]
    nb = T // Q_BLOCK
    F = jnp.cumsum(logf, axis=1)
    Ft = F.transpose(0, 2, 1)
    qb = q.reshape(B, nb, Q_BLOCK, FOX_HEADS, FOX_HEAD_DIM).transpose(1, 0, 2, 3, 4)
    Fb = F.reshape(B, nb, Q_BLOCK, FOX_HEADS).transpose(1, 0, 3, 2)
    kpos = jnp.arange(T)
    scale = FOX_HEAD_DIM ** -0.5

    def block(args):
        i, qi, Fq = args
        s = jnp.einsum('bqhd,bkhd->bhqk', qi, k, preferred_element_type=jnp.float32) * scale
        s = s + Fq[..., None] - Ft[:, :, None, :]
        qpos = i * Q_BLOCK + jnp.arange(Q_BLOCK)
        s = jnp.where(kpos[None, :] <= qpos[:, None], s, -jnp.inf)
        p = jax.nn.softmax(s, axis=-1).astype(v.dtype)
        return jnp.einsum('bhqk,bkhd->bqhd', p, v)

    out = lax.map(block, (jnp.arange(nb), qb, Fb))
    return out.transpose(1, 0, 2, 3, 4).reshape(B, T, FOX_WIDTH).astype(q.dtype)


def fox_sample(q, k, v, logf, cache_k, cache_v, cache_logf, page_table):
    S = q.shape[1]
    n_past = page_table.shape[1] * PAGE_SIZE
    mask = jnp.concatenate([jnp.ones((S, n_past), bool),
                            jnp.arange(S)[None, :] <= jnp.arange(S)[:, None]], axis=1)
    scale = FOX_HEAD_DIM ** -0.5

    def one(args):
        qi, ki, vi, lfi, pages = args
        kp = cache_k[pages].reshape(n_past, FOX_HEADS, FOX_HEAD_DIM).astype(ki.dtype)
        vp = cache_v[pages].reshape(n_past, FOX_HEADS, FOX_HEAD_DIM).astype(vi.dtype)
        lfp = cache_logf[pages].reshape(n_past, FOX_HEADS).astype(jnp.float32)
        Fp = jnp.cumsum(lfp, axis=0)
        Fn = Fp[-1] + jnp.cumsum(lfi, axis=0)
        kall = jnp.concatenate([kp, ki], axis=0)
        vall = jnp.concatenate([vp, vi], axis=0)
        Fall = jnp.concatenate([Fp, Fn], axis=0)
        s = jnp.einsum('qhd,khd->hqk', qi, kall, preferred_element_type=jnp.float32) * scale
        s = s + Fn.T[:, :, None] - Fall.T[:, None, :]
        s = jnp.where(mask, s, -jnp.inf)
        p = jax.nn.softmax(s, axis=-1).astype(vall.dtype)
        return jnp.einsum('hqk,khd->qhd', p, vall).reshape(S, FOX_WIDTH)

    return lax.map(one, (q, k, v, logf, page_table)).astype(q.dtype)


def short_conv(u, w_conv, prev):
    T = u.shape[1]
    ext = jnp.concatenate([prev.astype(u.dtype), u], axis=1)
    out = sum(ext[:, j:j + T] * w_conv[j] for j in range(CONV_K))
    return out, ext[:, ext.shape[1] - (CONV_K - 1):]


def merge(attn, conv, cb, ga, gb, w_fox_out, w_conv_out, w_o):
    ya = attn @ w_fox_out
    yb = (cb * conv) @ w_conv_out
    return (jax.nn.sigmoid(ga) * ya + jax.nn.sigmoid(gb) * yb) @ w_o


def memory_kv(mem, g_mem, w_xk, w_xv):
    B, M, _ = mem.shape
    mn = rmsnorm(mem, g_mem)
    mk = (mn @ w_xk).reshape(B, M, XA_HEADS, XA_HEAD_DIM)
    mv = (mn @ w_xv).reshape(B, M, XA_HEADS, XA_HEAD_DIM)
    return mk, mv


def cross_attn(xn, mk, mv, w_xq, w_xo):
    B, T, _ = xn.shape
    q = (xn @ w_xq).reshape(B, T, XA_HEADS, XA_HEAD_DIM)
    s = jnp.einsum('bthd,bmhd->bhtm', q, mk.astype(q.dtype), preferred_element_type=jnp.float32) * XA_HEAD_DIM ** -0.5
    p = jax.nn.softmax(s, axis=-1).astype(q.dtype)
    o = jnp.einsum('bhtm,bmhd->bthd', p, mv.astype(q.dtype))
    return o.reshape(B, T, D_MODEL) @ w_xo


def peer(xn, w_peer_q, keys_a, keys_b, expert_u, expert_v):
    B, T, D = xn.shape
    xf = xn.reshape(B * T, D)
    n = xf.shape[0]
    xf = jnp.pad(xf, ((0, (-n) % PEER_BLOCK), (0, 0)))
    xb = xf.reshape(-1, PEER_BLOCK, D)

    def block(xi):
        q = (xi @ w_peer_q).reshape(PEER_BLOCK, PEER_HEADS, 2, PEER_HALF)
        sa = jnp.einsum('nhd,hkd->nhk', q[:, :, 0], keys_a, preferred_element_type=jnp.float32)
        sb = jnp.einsum('nhd,hkd->nhk', q[:, :, 1], keys_b, preferred_element_type=jnp.float32)
        va, ia = lax.top_k(sa, PEER_TOPK)
        vb, ib = lax.top_k(sb, PEER_TOPK)
        cand = (va[..., :, None] + vb[..., None, :]).reshape(PEER_BLOCK, PEER_HEADS, PEER_TOPK * PEER_TOPK)
        cidx = (ia[..., :, None] * PEER_N_KEYS + ib[..., None, :]).reshape(PEER_BLOCK, PEER_HEADS, PEER_TOPK * PEER_TOPK)
        top_s, pos = lax.top_k(cand, PEER_TOPK)
        eidx = jnp.take_along_axis(cidx, pos, axis=-1)
        g = jax.nn.softmax(top_s, axis=-1)
        u = expert_u[eidx]
        act = jax.nn.gelu(jnp.einsum('nhkd,nd->nhk', u, xi, preferred_element_type=jnp.float32))
        w = (g * act).astype(xi.dtype)
        return jnp.einsum('nhk,nhkd->nd', w, expert_v[eidx])

    out = lax.map(block, xb).reshape(-1, D)[:n]
    return out.reshape(B, T, D).astype(xn.dtype)


def setup_inputs(seed: int = 0) -> dict:
    key = jax.random.key(seed)
    ks = jax.random.split(key, 40)
    nrm = lambda i, shape, scale: jax.random.normal(ks[i], shape, jnp.float32) * scale
    gain = lambda i, shape: 1.0 + 0.05 * jax.random.normal(ks[i], shape, jnp.float32)
    n_pages = PAST_LEN // PAGE_SIZE
    used = DEC_BATCH * n_pages
    n_pool = used + max(1, used // 4)
    page_table = jax.random.permutation(ks[0], n_pool)[:used].reshape(DEC_BATCH, n_pages).astype(jnp.int32)
    return {
        'x_prompt': nrm(1, (BATCH, SEQ, D_MODEL), 1.0),
        'x_sample': nrm(2, (DEC_BATCH, DEC_SEQ, D_MODEL), 1.0),
        'cache_fox_k': nrm(3, (DEPTH, n_pool, PAGE_SIZE, FOX_HEADS, FOX_HEAD_DIM), 1.0),
        'cache_fox_v': nrm(4, (DEPTH, n_pool, PAGE_SIZE, FOX_HEADS, FOX_HEAD_DIM), 1.0),
        'cache_fox_logf': jax.nn.log_sigmoid(FORGET_BIAS + nrm(5, (DEPTH, n_pool, PAGE_SIZE, FOX_HEADS), 1.0)),
        'cache_mem_k': nrm(6, (DEPTH, DEC_BATCH, MEM_LEN, XA_HEADS, XA_HEAD_DIM), 1.0),
        'cache_mem_v': nrm(7, (DEPTH, DEC_BATCH, MEM_LEN, XA_HEADS, XA_HEAD_DIM), 1.0),
        'state_conv': nrm(8, (DEPTH, DEC_BATCH, CONV_K - 1, CONV_WIDTH), 1.0),
        'page_table': page_table,
        'mem_prompt': nrm(9, (BATCH, MEM_LEN, D_MODEL), 1.0),
        'g_mix': gain(10, (DEPTH, D_MODEL)),
        'w_in': nrm(11, (DEPTH, D_MODEL, IN_PROJ_WIDTH), D_MODEL ** -0.5),
        'b_forget': FORGET_BIAS + nrm(12, (DEPTH, FOX_HEADS), 1.0),
        'w_conv': nrm(13, (DEPTH, CONV_K, CONV_WIDTH), CONV_K ** -0.5),
        'w_fox_out': nrm(14, (DEPTH, FOX_WIDTH, D_MODEL), FOX_WIDTH ** -0.5),
        'w_conv_out': nrm(15, (DEPTH, CONV_WIDTH, D_MODEL), CONV_WIDTH ** -0.5),
        'w_o': nrm(16, (DEPTH, D_MODEL, D_MODEL), D_MODEL ** -0.5),
        'g_xattn': gain(17, (DEPTH, D_MODEL)),
        'g_mem': gain(18, (DEPTH, D_MODEL)),
        'w_xq': nrm(19, (DEPTH, D_MODEL, D_MODEL), D_MODEL ** -0.5),
        'w_xk': nrm(20, (DEPTH, D_MODEL, D_MODEL), D_MODEL ** -0.5),
        'w_xv': nrm(21, (DEPTH, D_MODEL, D_MODEL), D_MODEL ** -0.5),
        'w_xo': nrm(22, (DEPTH, D_MODEL, D_MODEL), D_MODEL ** -0.5),
        'g_peer': gain(23, (DEPTH, D_MODEL)),
        'w_peer_q': nrm(24, (DEPTH, D_MODEL, PEER_HEADS * PEER_KEY_DIM), D_MODEL ** -0.5),
        'peer_keys_a': nrm(25, (DEPTH, PEER_HEADS, PEER_N_KEYS, PEER_HALF), PEER_HALF ** -0.5),
        'peer_keys_b': nrm(26, (DEPTH, PEER_HEADS, PEER_N_KEYS, PEER_HALF), PEER_HALF ** -0.5),
        'peer_u': nrm(27, (DEPTH, PEER_N_EXPERTS, D_MODEL), D_MODEL ** -0.5),
        'peer_v': nrm(28, (DEPTH, PEER_N_EXPERTS, D_MODEL), PEER_HEADS ** -0.5),
        'g_final': gain(29, (D_MODEL,)),
    }


def reference(x_prompt, x_sample, cache_fox_k, cache_fox_v, cache_fox_logf, cache_mem_k, cache_mem_v,
              state_conv, page_table, mem_prompt, g_mix, w_in, b_forget, w_conv, w_fox_out, w_conv_out,
              w_o, g_xattn, g_mem, w_xq, w_xk, w_xv, w_xo, g_peer, w_peer_q, peer_keys_a, peer_keys_b,
              peer_u, peer_v, g_final):
    hp, hs = x_prompt, x_sample
    kp_l, vp_l, fp_l, mkp_l, mvp_l, cp_l = [], [], [], [], [], []
    ks_l, vs_l, fs_l, cs_l = [], [], [], []
    for l in range(DEPTH):
        q, k, v, logf, u, cb, ga, gb = mix_inputs(rmsnorm(hp, g_mix[l]), w_in[l], b_forget[l])
        attn = fox_prompt(q, k, v, logf)
        conv, conv_state = short_conv(u, w_conv[l], jnp.zeros((hp.shape[0], CONV_K - 1, CONV_WIDTH), u.dtype))
        hp = hp + merge(attn, conv, cb, ga, gb, w_fox_out[l], w_conv_out[l], w_o[l])
        mk, mv = memory_kv(mem_prompt, g_mem[l], w_xk[l], w_xv[l])
        hp = hp + cross_attn(rmsnorm(hp, g_xattn[l]), mk, mv, w_xq[l], w_xo[l])
        hp = hp + peer(rmsnorm(hp, g_peer[l]), w_peer_q[l], peer_keys_a[l], peer_keys_b[l], peer_u[l], peer_v[l])
        kp_l.append(k); vp_l.append(v); fp_l.append(logf); mkp_l.append(mk); mvp_l.append(mv); cp_l.append(conv_state)

        q, k, v, logf, u, cb, ga, gb = mix_inputs(rmsnorm(hs, g_mix[l]), w_in[l], b_forget[l])
        attn = fox_sample(q, k, v, logf, cache_fox_k[l], cache_fox_v[l], cache_fox_logf[l], page_table)
        conv, conv_state = short_conv(u, w_conv[l], state_conv[l])
        hs = hs + merge(attn, conv, cb, ga, gb, w_fox_out[l], w_conv_out[l], w_o[l])
        hs = hs + cross_attn(rmsnorm(hs, g_xattn[l]), cache_mem_k[l], cache_mem_v[l], w_xq[l], w_xo[l])
        hs = hs + peer(rmsnorm(hs, g_peer[l]), w_peer_q[l], peer_keys_a[l], peer_keys_b[l], peer_u[l], peer_v[l])
        ks_l.append(k); vs_l.append(v); fs_l.append(logf); cs_l.append(conv_state)

    y_prompt = rmsnorm(hp, g_final)
    y_sample = rmsnorm(hs, g_final)
    return (y_prompt, y_sample,
            jnp.stack(kp_l), jnp.stack(vp_l), jnp.stack(fp_l), jnp.stack(mkp_l), jnp.stack(mvp_l), jnp.stack(cp_l),
            jnp.stack(ks_l), jnp.stack(vs_l), jnp.stack(fs_l), jnp.stack(cs_l))
```

```python
import functools

import jax
import jax.numpy as jnp
from jax import lax
from jax.experimental import pallas as pl
from jax.experimental.pallas import tpu as pltpu
from jax.experimental.pallas import tpu_sc as plsc

F32 = jnp.float32
BF16 = jnp.bfloat16
I32 = jnp.int32

D_MODEL = 1024
FOX_HEADS = 8
FOX_HEAD_DIM = 64
FOX_WIDTH = FOX_HEADS * FOX_HEAD_DIM
CONV_WIDTH = D_MODEL // 2
CONV_K = 3
PAGE_SIZE = 128
XA_HEADS = 4
XA_HEAD_DIM = D_MODEL // XA_HEADS
PEER_HEADS = 8
PEER_N_KEYS = 128
PEER_HALF = 64
PEER_TOPK = 16
PEER_SLOTS = PEER_HEADS * PEER_TOPK
RMS_EPS = 1e-6

LANES = 128
SUBLANES = 8
VMEM_LIMIT = 56 * 1024 * 1024
NEG_INF = float("-inf")


def _cparams(sem):
    return pltpu.CompilerParams(dimension_semantics=sem, vmem_limit_bytes=VMEM_LIMIT)


def _full(shape):
    return pl.BlockSpec(shape, lambda *_: (0,) * len(shape))


def _rmsnorm(x, g):
    return x * lax.rsqrt(jnp.mean(x * x, axis=-1, keepdims=True) + RMS_EPS) * g


def _log_sigmoid(x):
    return jnp.minimum(x, 0.0) - jnp.log1p(jnp.exp(-jnp.abs(x)))


def _in_proj_body(x_ref, g_ref, wqkv_ref, wf_ref, wc_ref, wg_ref, bf_ref, wconv_ref, prev_ref,
                  q_ref, kf_ref, vf_ref, kb_ref, vb_ref, logf_ref, cbz_ref, sga_ref, sgb_ref, utail_ref,
                  carry_ref, *, seq_rows):
    tm = x_ref.shape[0]
    xn = _rmsnorm(x_ref[...], g_ref[...]).astype(BF16)

    qkv = jnp.dot(xn, wqkv_ref[...], preferred_element_type=F32)
    k = qkv[:, FOX_WIDTH:2 * FOX_WIDTH]
    v = qkv[:, 2 * FOX_WIDTH:]
    q_ref[...] = (qkv[:, :FOX_WIDTH] * (FOX_HEAD_DIM ** -0.5)).astype(BF16)
    kf_ref[...] = k
    vf_ref[...] = v
    kb_ref[...] = k.astype(BF16)
    vb_ref[...] = v.astype(BF16)

    fl = jnp.dot(xn, wf_ref[...], preferred_element_type=F32)
    logf_ref[...] = _log_sigmoid(fl[:, :FOX_HEADS] + bf_ref[...])

    c3 = jnp.dot(xn, wc_ref[...], preferred_element_type=F32)
    u = c3[:, CONV_WIDTH:2 * CONV_WIDTH] * c3[:, :CONV_WIDTH]
    cb = c3[:, 2 * CONV_WIDTH:]
    rows = lax.broadcasted_iota(I32, (tm, CONV_WIDTH), 0)
    if seq_rows is None:
        @pl.when(pl.program_id(1) == 0)
        def _():
            carry_ref[...] = prev_ref[...]
        hist = carry_ref[...]
        u1 = jnp.where(rows == 0, hist[7:8], pltpu.roll(u, 1, 0))
        u2 = jnp.where(rows == 0, hist[6:7], jnp.where(rows == 1, hist[7:8], pltpu.roll(u, 2, 0)))
        carry_ref[...] = u[tm - SUBLANES:]
        utail_ref[...] = u[tm - SUBLANES:]
    else:
        t = rows % seq_rows
        prev = prev_ref[...]
        u1 = jnp.where(t == 0, pltpu.roll(prev, tm - (seq_rows - 1), 0), pltpu.roll(u, 1, 0))
        u2 = jnp.where(t < 2, pltpu.roll(prev, tm - (seq_rows - 2), 0), pltpu.roll(u, 2, 0))
        utail_ref[...] = u
    wconv = wconv_ref[...]
    conv = u2 * wconv[0:1] + u1 * wconv[1:2] + u * wconv[2:3]
    cbz_ref[...] = (cb * conv).astype(BF16)

    gates = jnp.dot(xn, wg_ref[...], preferred_element_type=F32)
    sga_ref[...] = jax.nn.sigmoid(gates[:, :D_MODEL])
    sgb_ref[...] = jax.nn.sigmoid(gates[:, D_MODEL:])


def _prep_in_weights(w_in):
    o_f = 3 * FOX_WIDTH
    o_c = o_f + FOX_HEADS
    o_g = o_c + 3 * CONV_WIDTH
    wqkv = w_in[:, :o_f].astype(BF16)
    wf = jnp.pad(w_in[:, o_f:o_c], ((0, 0), (0, LANES - FOX_HEADS))).astype(BF16)
    wc = w_in[:, o_c:o_g].astype(BF16)
    wg = w_in[:, o_g:].astype(BF16)
    return wqkv, wf, wc, wg


def _in_proj(x2, g_mix, wqkv, wf, wc, wg, b_forget, w_conv, prev8, *, n_seq, seq_len, tm):
    n = x2.shape[0]
    if seq_len >= tm:
        assert seq_len % tm == 0
        nt = seq_len // tm
        grid = (n_seq, nt)
        tok = lambda w: pl.BlockSpec((tm, w), lambda b, t: (b * nt + t, 0))
        seq8 = pl.BlockSpec((SUBLANES, CONV_WIDTH), lambda b, t: (b, 0))
        seq_rows = None
    else:
        assert seq_len == SUBLANES and tm == n
        grid = (1, 1)
        tok = lambda w: pl.BlockSpec((tm, w), lambda b, t: (0, 0))
        seq8 = pl.BlockSpec((tm, CONV_WIDTH), lambda b, t: (0, 0))
        seq_rows = seq_len
    out_shape = (
        jax.ShapeDtypeStruct((n, FOX_WIDTH), BF16),
        jax.ShapeDtypeStruct((n, FOX_WIDTH), F32),
        jax.ShapeDtypeStruct((n, FOX_WIDTH), F32),
        jax.ShapeDtypeStruct((n, FOX_WIDTH), BF16),
        jax.ShapeDtypeStruct((n, FOX_WIDTH), BF16),
        jax.ShapeDtypeStruct((n, FOX_HEADS), F32),
        jax.ShapeDtypeStruct((n, CONV_WIDTH), BF16),
        jax.ShapeDtypeStruct((n, D_MODEL), F32),
        jax.ShapeDtypeStruct((n, D_MODEL), F32),
        jax.ShapeDtypeStruct((n_seq * SUBLANES, CONV_WIDTH), F32),
    )
    out_specs = (tok(FOX_WIDTH), tok(FOX_WIDTH), tok(FOX_WIDTH), tok(FOX_WIDTH), tok(FOX_WIDTH),
                 tok(FOX_HEADS), tok(CONV_WIDTH), tok(D_MODEL), tok(D_MODEL), seq8)
    in_specs = [tok(D_MODEL), _full((1, D_MODEL)), _full(wqkv.shape), _full(wf.shape), _full(wc.shape),
                _full(wg.shape), _full((1, FOX_HEADS)), _full((CONV_K, CONV_WIDTH)), seq8]
    return pl.pallas_call(
        functools.partial(_in_proj_body, seq_rows=seq_rows),
        grid=grid, in_specs=in_specs, out_specs=out_specs, out_shape=out_shape,
        scratch_shapes=[pltpu.VMEM((SUBLANES, CONV_WIDTH), F32)],
        compiler_params=_cparams(("arbitrary", "arbitrary")),
        name="in_proj",
    )(x2, g_mix.reshape(1, D_MODEL), wqkv, wf, wc, wg, b_forget.reshape(1, FOX_HEADS), w_conv, prev8)


def _lane_prefix_scan(x, lane):
    s = x
    sh = 1
    while sh < LANES:
        s = s + jnp.where(lane >= sh, pltpu.roll(s, sh, 1), 0.0)
        sh *= 2
    return s


def _cumsum_body(x_ref, o_ref):
    n_chunks = x_ref.shape[2] // LANES
    lane = lax.broadcasted_iota(I32, (FOX_HEADS, LANES), 1)

    def step(c, carry):
        off = pl.multiple_of(c * LANES, LANES)
        s = _lane_prefix_scan(x_ref[0, :, pl.ds(off, LANES)], lane) + carry
        o_ref[0, :, pl.ds(off, LANES)] = s
        return s[:, LANES - 1:]

    lax.fori_loop(0, n_chunks, step, jnp.zeros((FOX_HEADS, 1), F32))


def _cumsum_time(logf_t):
    b, h, t = logf_t.shape
    spec = pl.BlockSpec((1, h, t), lambda i: (i, 0, 0))
    return pl.pallas_call(
        _cumsum_body, grid=(b,), in_specs=[spec], out_specs=spec,
        out_shape=jax.ShapeDtypeStruct(logf_t.shape, F32),
        compiler_params=_cparams(("arbitrary",)), name="forget_cumsum",
    )(logf_t)


def _fox_prompt_body(q_ref, k_ref, v_ref, fcol_ref, frow_ref, o_ref, *, tq):
    hp = pl.program_id(1)
    qi = pl.program_id(2)
    q2 = q_ref[0]
    lane = lax.broadcasted_iota(I32, (tq, LANES), 1)
    upper = lane >= FOX_HEAD_DIM
    col8 = lax.broadcasted_iota(I32, (tq, FOX_HEADS), 1)
    fcol8 = fcol_ref[0]
    rowi = lax.broadcasted_iota(I32, (tq, tq), 0)
    coli = lax.broadcasted_iota(I32, (tq, tq), 1)
    zero = jnp.zeros_like(q2)

    outs = []
    for hh in range(2):
        h = 2 * hp + hh
        qh = jnp.where(upper, q2, zero) if hh else jnp.where(upper, zero, q2)
        fq = jnp.sum(jnp.where(col8 == h, fcol8, 0.0), axis=1, keepdims=True)

        def block(kb, carry, masked):
            m, l, acc = carry
            off = pl.multiple_of(kb * tq, tq)
            kk = k_ref[0, pl.ds(off, tq), :]
            vv = v_ref[0, pl.ds(off, tq), :]
            fk = frow_ref[0, pl.ds(h, 1), pl.ds(off, tq)]
            s = lax.dot_general(qh, kk, (((1,), (1,)), ((), ())), preferred_element_type=F32)
            s = s + fq - fk
            if masked:
                s = jnp.where(coli <= rowi, s, NEG_INF)
            m_new = jnp.maximum(m, jnp.max(s, axis=1, keepdims=True))
            p = jnp.exp(s - m_new)
            alpha = jnp.exp(m - m_new)
            l = alpha * l + jnp.sum(p, axis=1, keepdims=True)
            acc = alpha * acc + jnp.dot(p.astype(BF16), vv, preferred_element_type=F32)
            return m_new, l, acc

        init = (jnp.full((tq, 1), NEG_INF, F32), jnp.zeros((tq, 1), F32), jnp.zeros((tq, LANES), F32))
        carry = lax.fori_loop(0, qi, functools.partial(block, masked=False), init)
        m, l, acc = block(qi, carry, True)
        outs.append(acc / l)
    o_ref[0] = jnp.where(upper, outs[1], outs[0]).astype(o_ref.dtype)


def _fox_prompt(q, k, v, fcol, frow, *, tq):
    b, t, _ = q.shape
    n_pairs = FOX_WIDTH // LANES
    return pl.pallas_call(
        functools.partial(_fox_prompt_body, tq=tq),
        grid=(b, n_pairs, t // tq),
        in_specs=[
            pl.BlockSpec((1, tq, LANES), lambda i, p, j: (i, j, p)),
            pl.BlockSpec((1, t, LANES), lambda i, p, j: (i, 0, p)),
            pl.BlockSpec((1, t, LANES), lambda i, p, j: (i, 0, p)),
            pl.BlockSpec((1, tq, FOX_HEADS), lambda i, p, j: (i, j, 0)),
            pl.BlockSpec((1, FOX_HEADS, t), lambda i, p, j: (i, 0, 0)),
        ],
        out_specs=pl.BlockSpec((1, tq, LANES), lambda i, p, j: (i, j, p)),
        out_shape=jax.ShapeDtypeStruct((b, t, FOX_WIDTH), BF16),
        compiler_params=_cparams(("arbitrary", "arbitrary", "arbitrary")),
        name="fox_prompt",
    )(q, k, v, fcol, frow)


SAMPLE_ROWS = 64


def _lane_suffix_excl(x, lane):
    s = jnp.where(lane < LANES - 1, pltpu.roll(x, LANES - 1, 1), 0.0)
    sh = 1
    while sh < LANES:
        s = s + jnp.where(lane < LANES - sh, pltpu.roll(s, LANES - sh, 1), 0.0)
        sh *= 2
    return s


def _fox_sample_body(pt_ref, q_ref, kn_ref, vn_ref, lfcol_ref, lfrow_ref, *rest, pages_per_step):
    pg = pages_per_step
    k_refs = rest[:pg]
    v_refs = rest[pg:2 * pg]
    lf_refs = rest[2 * pg:3 * pg]
    o_ref = rest[3 * pg]
    m_ref, l_ref, acc_ref, tot_ref, qbd_ref, rt_ref = rest[3 * pg + 1:]
    g = pl.program_id(1)
    lane8 = lax.broadcasted_iota(I32, (FOX_HEADS, LANES), 1)
    row = lax.broadcasted_iota(I32, (SAMPLE_ROWS, FOX_WIDTH), 0)
    lane = lax.broadcasted_iota(I32, (SAMPLE_ROWS, FOX_WIDTH), 1)
    head_mask = (lane // FOX_HEAD_DIM) == (row % FOX_HEADS)

    def tile8(x):
        return jnp.concatenate([x] * (SAMPLE_ROWS // FOX_HEADS), axis=0)

    def update(s, vcat):
        m_old = m_ref[...]
        m_new = jnp.maximum(m_old, jnp.max(s, axis=1, keepdims=True))
        p = jnp.exp(s - m_new)
        alpha = jnp.exp(m_old - m_new)
        l_ref[...] = alpha * l_ref[...] + jnp.sum(p, axis=1, keepdims=True)
        acc_ref[...] = alpha * acc_ref[...] + jnp.dot(p.astype(BF16), vcat, preferred_element_type=F32)
        m_ref[...] = m_new

    @pl.when(g == 0)
    def _():
        q = q_ref[0]
        qrows = jnp.concatenate(
            [jnp.broadcast_to(q[t:t + 1, :], (FOX_HEADS, FOX_WIDTH)) for t in range(q.shape[0])], axis=0)
        qbd = jnp.where(head_mask, qrows, jnp.zeros_like(qrows))
        qbd_ref[...] = qbd
        r1 = lax.broadcasted_iota(I32, (SAMPLE_ROWS, 1), 0)
        x = lfcol_ref[0]
        s = jnp.where(r1 < SAMPLE_ROWS - 8, pltpu.roll(x, SAMPLE_ROWS - 8, 0), 0.0)
        for sh in (8, 16, 32):
            s = s + jnp.where(r1 < SAMPLE_ROWS - sh, pltpu.roll(s, SAMPLE_ROWS - sh, 0), 0.0)
        rt = -s
        rt_ref[...] = rt
        lfn = lfrow_ref[0]
        excl = _lane_suffix_excl(lfn, lane8)
        tot_ref[...] = excl[:, 0:1] + lfn[:, 0:1]
        sc = lax.dot_general(qbd, kn_ref[0], (((1,), (1,)), ((), ())), preferred_element_type=F32)
        sc = sc + rt + tile8(excl)
        rr = lax.broadcasted_iota(I32, (SAMPLE_ROWS, LANES), 0)
        cc = lax.broadcasted_iota(I32, (SAMPLE_ROWS, LANES), 1)
        sc = jnp.where(cc <= rr // FOX_HEADS, sc, NEG_INF)
        m_new = jnp.max(sc, axis=1, keepdims=True)
        p = jnp.exp(sc - m_new)
        m_ref[...] = m_new
        l_ref[...] = jnp.sum(p, axis=1, keepdims=True)
        acc_ref[...] = jnp.dot(p.astype(BF16), vn_ref[0], preferred_element_type=F32)

    qbd = qbd_ref[...]
    tot = tot_ref[...]
    biases, ks, vs = [], [], []
    for i in reversed(range(pg)):
        lfp = lf_refs[i][0]
        r = _lane_suffix_excl(lfp, lane8) + tot
        tot = r[:, 0:1] + lfp[:, 0:1]
        biases.append(tile8(r))
        ks.append(k_refs[i][0].astype(BF16))
        vs.append(v_refs[i][0].astype(BF16))
    tot_ref[...] = tot
    kcat = jnp.concatenate(ks, axis=0)
    vcat = jnp.concatenate(vs, axis=0)
    sc = lax.dot_general(qbd, kcat, (((1,), (1,)), ((), ())), preferred_element_type=F32)
    sc = sc + rt_ref[...] + jnp.concatenate(biases, axis=1)
    update(sc, vcat)

    @pl.when(g == pl.num_programs(1) - 1)
    def _():
        full = jnp.where(head_mask, acc_ref[...] / l_ref[...], 0.0)
        o_ref[0] = jnp.concatenate(
            [jnp.sum(full[FOX_HEADS * t:FOX_HEADS * (t + 1)], axis=0, keepdims=True)
             for t in range(SAMPLE_ROWS // FOX_HEADS)], axis=0).astype(o_ref.dtype)


def _fox_sample(page_table, q, kn_pad, vn_pad, lfcol, lfrow_pad, cache_k, cache_v, cache_lf_t, *, pages_per_step):
    bd, n_pages = page_table.shape
    pg = pages_per_step
    assert n_pages % pg == 0
    ng = n_pages // pg
    s_new = q.shape[1]

    def page_spec(shape, i):
        return pl.BlockSpec(shape, lambda b, g, pt: (pt[b, (ng - 1 - g) * pg + i], 0, 0))

    req = lambda shape: pl.BlockSpec(shape, lambda b, g, pt: (b, 0, 0))
    in_specs = [req((1, s_new, FOX_WIDTH)), req((1, PAGE_SIZE, FOX_WIDTH)), req((1, PAGE_SIZE, FOX_WIDTH)),
                req((1, SAMPLE_ROWS, 1)), req((1, FOX_HEADS, LANES))]
    in_specs += [page_spec((1, PAGE_SIZE, FOX_WIDTH), i) for i in range(pg)]
    in_specs += [page_spec((1, PAGE_SIZE, FOX_WIDTH), i) for i in range(pg)]
    in_specs += [page_spec((1, FOX_HEADS, PAGE_SIZE), i) for i in range(pg)]
    grid_spec = pltpu.PrefetchScalarGridSpec(
        num_scalar_prefetch=1, grid=(bd, ng), in_specs=in_specs,
        out_specs=req((1, s_new, FOX_WIDTH)),
        scratch_shapes=[pltpu.VMEM((SAMPLE_ROWS, 1), F32), pltpu.VMEM((SAMPLE_ROWS, 1), F32),
                        pltpu.VMEM((SAMPLE_ROWS, FOX_WIDTH), F32), pltpu.VMEM((FOX_HEADS, 1), F32),
                        pltpu.VMEM((SAMPLE_ROWS, FOX_WIDTH), BF16), pltpu.VMEM((SAMPLE_ROWS, 1), F32)])
    return pl.pallas_call(
        functools.partial(_fox_sample_body, pages_per_step=pg),
        grid_spec=grid_spec, out_shape=jax.ShapeDtypeStruct((bd, s_new, FOX_WIDTH), BF16),
        compiler_params=_cparams(("arbitrary", "arbitrary")), name="fox_sample",
    )(page_table, q, kn_pad, vn_pad, lfcol, lfrow_pad, *([cache_k] * pg), *([cache_v] * pg), *([cache_lf_t] * pg))


def _merge_body(x_ref, attn_ref, cbz_ref, sga_ref, sgb_ref, wfo_ref, wco_ref, wo_ref, gx_ref, wxq_ref,
                h1_ref, qx_ref):
    ya = jnp.dot(attn_ref[...], wfo_ref[...], preferred_element_type=F32)
    yb = jnp.dot(cbz_ref[...], wco_ref[...], preferred_element_type=F32)
    mix = (sga_ref[...] * ya + sgb_ref[...] * yb).astype(BF16)
    h1 = x_ref[...] + jnp.dot(mix, wo_ref[...], preferred_element_type=F32)
    h1_ref[...] = h1
    xn = _rmsnorm(h1, gx_ref[...]).astype(BF16)
    qx_ref[...] = (jnp.dot(xn, wxq_ref[...], preferred_element_type=F32) * (XA_HEAD_DIM ** -0.5)).astype(BF16)


def _merge(x2, attn, cbz, sga, sgb, wfo, wco, wo, g_xattn, wxq, *, tm):
    n = x2.shape[0]
    tok = lambda w: pl.BlockSpec((tm, w), lambda i: (i, 0))
    return pl.pallas_call(
        _merge_body, grid=(n // tm,),
        in_specs=[tok(D_MODEL), tok(FOX_WIDTH), tok(CONV_WIDTH), tok(D_MODEL), tok(D_MODEL),
                  _full(wfo.shape), _full(wco.shape), _full(wo.shape), _full((1, D_MODEL)), _full(wxq.shape)],
        out_specs=(tok(D_MODEL), tok(D_MODEL)),
        out_shape=(jax.ShapeDtypeStruct((n, D_MODEL), F32), jax.ShapeDtypeStruct((n, D_MODEL), BF16)),
        compiler_params=_cparams(("arbitrary",)), name="merge",
    )(x2, attn, cbz, sga, sgb, wfo, wco, wo, g_xattn.reshape(1, D_MODEL), wxq)


def _memory_kv_body(m_ref, g_ref, wk_ref, wv_ref, mk_ref, mv_ref):
    mn = _rmsnorm(m_ref[...], g_ref[...]).astype(BF16)
    mk_ref[...] = jnp.dot(mn, wk_ref[...], preferred_element_type=F32)
    mv_ref[...] = jnp.dot(mn, wv_ref[...], preferred_element_type=F32)


def _memory_kv(mem2, g_mem, wxk, wxv, *, tm):
    n = mem2.shape[0]
    tok = pl.BlockSpec((tm, D_MODEL), lambda i: (i, 0))
    return pl.pallas_call(
        _memory_kv_body, grid=(n // tm,),
        in_specs=[tok, _full((1, D_MODEL)), _full(wxk.shape), _full(wxv.shape)],
        out_specs=(tok, tok),
        out_shape=(jax.ShapeDtypeStruct((n, D_MODEL), F32),) * 2,
        compiler_params=_cparams(("arbitrary",)), name="memory_kv",
    )(mem2, g_mem.reshape(1, D_MODEL), wxk, wxv)


def _cross_attn_body(q_ref, mk_ref, mv_ref, o_ref):
    q = q_ref[0]
    outs = []
    for h in range(XA_HEADS):
        sl = slice(h * XA_HEAD_DIM, (h + 1) * XA_HEAD_DIM)
        kh = mk_ref[0, :, sl].astype(BF16)
        vh = mv_ref[0, :, sl].astype(BF16)
        s = lax.dot_general(q[:, sl], kh, (((1,), (1,)), ((), ())), preferred_element_type=F32)
        p = jnp.exp(s - jnp.max(s, axis=1, keepdims=True))
        p = p / jnp.sum(p, axis=1, keepdims=True)
        outs.append(jnp.dot(p.astype(BF16), vh, preferred_element_type=F32))
    o_ref[0] = jnp.concatenate(outs, axis=1).astype(o_ref.dtype)


def _cross_attn(qx, mk, mv, *, tm):
    b, t, _ = qx.shape
    m = mk.shape[1]
    tok = pl.BlockSpec((1, tm, D_MODEL), lambda i, j: (i, j, 0))
    mem = pl.BlockSpec((1, m, D_MODEL), lambda i, j: (i, 0, 0))
    return pl.pallas_call(
        _cross_attn_body, grid=(b, t // tm), in_specs=[tok, mem, mem], out_specs=tok,
        out_shape=jax.ShapeDtypeStruct((b, t, D_MODEL), BF16),
        compiler_params=_cparams(("arbitrary", "arbitrary")), name="cross_attn",
    )(qx, mk, mv)


PEER_PAIRS = tuple((i, j) for i in range(PEER_TOPK) for j in range(PEER_TOPK) if (i + 1) * (j + 1) <= PEER_TOPK)
PEER_CAND_ROWS = -(-len(PEER_PAIRS) // SUBLANES) * SUBLANES


def _top_rows(s, k, payload=None):
    n = s.shape[0]
    iota = lax.broadcasted_iota(I32, s.shape, 0)
    vals, ids = [], []
    for _ in range(k):
        m = jnp.max(s, axis=0, keepdims=True)
        r = jnp.min(jnp.where(s == m, iota, n), axis=0, keepdims=True)
        hit = iota == r
        vals.append(m)
        ids.append(r if payload is None else jnp.max(jnp.where(hit, payload, -1), axis=0, keepdims=True))
        s = jnp.where(hit, NEG_INF, s)
    return vals, ids


def _peer_front_body(h1_ref, o_ref, wxo_ref, gp_ref, wpq_ref, ka_ref, kb_ref,
                     h2_ref, xn_ref, eidx_ref, gate_ref, qp_ref, e_ref, g_ref):
    tm = h1_ref.shape[0]
    h2 = h1_ref[...] + jnp.dot(o_ref[...], wxo_ref[...], preferred_element_type=F32)
    h2_ref[...] = h2
    xn = _rmsnorm(h2, gp_ref[...]).astype(BF16)
    xn_ref[...] = xn
    qp_ref[...] = jnp.dot(xn, wpq_ref[...], preferred_element_type=F32).astype(BF16)

    def head(h, _):
        qh = qp_ref[:, pl.ds(pl.multiple_of(h * LANES, LANES), LANES)]
        nt = (((1,), (1,)), ((), ()))
        sa = lax.dot_general(ka_ref[h], qh, nt, preferred_element_type=F32)
        sb = lax.dot_general(kb_ref[h], qh, nt, preferred_element_type=F32)
        va, ia = _top_rows(sa, PEER_TOPK)
        vb, ib = _top_rows(sb, PEER_TOPK)
        pad = PEER_CAND_ROWS - len(PEER_PAIRS)
        cand = jnp.concatenate([va[i] + vb[j] for i, j in PEER_PAIRS]
                               + [jnp.full((pad, tm), NEG_INF, F32)], axis=0)
        ceid = jnp.concatenate([ia[i] * PEER_N_KEYS + ib[j] for i, j in PEER_PAIRS]
                               + [jnp.zeros((pad, tm), I32)], axis=0)
        ts, te = _top_rows(cand, PEER_TOPK, payload=ceid)
        ex = [jnp.exp(t - ts[0]) for t in ts]
        den = ex[0]
        for e in ex[1:]:
            den = den + e
        row0 = pl.multiple_of(h * PEER_TOPK, PEER_TOPK)
        e_ref[pl.ds(row0, PEER_TOPK), :] = jnp.concatenate(te, axis=0)
        g_ref[pl.ds(row0, PEER_TOPK), :] = jnp.concatenate([e / den for e in ex], axis=0)
        return 0

    lax.fori_loop(0, PEER_HEADS, head, 0)
    eidx_ref[0] = e_ref[...]
    gate_ref[...] = g_ref[...].T


def _prep_peer_keys(keys_a, keys_b):
    z = jnp.zeros_like(keys_a)
    return (jnp.concatenate([keys_a, z], axis=-1).astype(BF16), jnp.concatenate([z, keys_b], axis=-1).astype(BF16))


def _peer_front(h1, o, wxo, g_peer, wpq, ka_pad, kb_pad, *, tm):
    n = h1.shape[0]
    tok = lambda w: pl.BlockSpec((tm, w), lambda i: (i, 0))
    return pl.pallas_call(
        _peer_front_body, grid=(n // tm,),
        in_specs=[tok(D_MODEL), tok(D_MODEL), _full(wxo.shape), _full((1, D_MODEL)), _full(wpq.shape),
                  _full(ka_pad.shape), _full(kb_pad.shape)],
        out_specs=(tok(D_MODEL), tok(D_MODEL), pl.BlockSpec((1, PEER_SLOTS, tm), lambda i: (i, 0, 0)),
                   tok(PEER_SLOTS)),
        out_shape=(jax.ShapeDtypeStruct((n, D_MODEL), F32), jax.ShapeDtypeStruct((n, D_MODEL), BF16),
                   jax.ShapeDtypeStruct((n // tm, PEER_SLOTS, tm), I32),
                   jax.ShapeDtypeStruct((n, PEER_SLOTS), F32)),
        scratch_shapes=[pltpu.VMEM((tm, PEER_HEADS * LANES), BF16), pltpu.VMEM((PEER_SLOTS, tm), I32),
                        pltpu.VMEM((PEER_SLOTS, tm), F32)],
        compiler_params=_cparams(("arbitrary",)), name="peer_front",
    )(h1, o, wxo, g_peer.reshape(1, D_MODEL), wpq, ka_pad, kb_pad)


PACKED_WIDTH = D_MODEL // 2


def _pack_table(t):
    b = lax.bitcast_convert_type(t.astype(BF16), jnp.uint16).astype(jnp.uint32)
    return lax.bitcast_convert_type(b[:, :PACKED_WIDTH] | (b[:, PACKED_WIDTH:] << 16), I32)


def _unpack(w):
    lo = lax.bitcast_convert_type(w << 16, F32)
    hi = lax.bitcast_convert_type(w & jnp.int32(-65536), F32)
    return lo, hi


SC_CORES = 2
SC_SUBCORES = 16
SC_WINDOW = 64


def _sc_gather2(table_u, table_v, idx):
    r = idx.shape[0]
    workers = SC_CORES * SC_SUBCORES
    assert r % (workers * SC_WINDOW) == 0
    per_worker = r // workers
    n_win = per_worker // SC_WINDOW
    width = table_u.shape[1]
    mesh = plsc.VectorSubcoreMesh(core_axis_name="c", subcore_axis_name="s")
    out = jax.ShapeDtypeStruct((r, width), table_u.dtype)

    @functools.partial(
        pl.kernel, mesh=mesh, out_type=(out, out),
        scratch_types=[pltpu.VMEM((SC_WINDOW,), I32), pltpu.VMEM((SC_WINDOW, width), table_u.dtype),
                       pltpu.VMEM((SC_WINDOW, width), table_u.dtype),
                       pltpu.SemaphoreType.DMA, pltpu.SemaphoreType.DMA])
    def gather(u_hbm, v_hbm, idx_hbm, ou_hbm, ov_hbm, idx_v, ru_v, rv_v, sem_u, sem_v):
        wid = lax.axis_index("s") * SC_CORES + lax.axis_index("c")
        base = wid * per_worker

        @pl.loop(0, n_win)
        def _(i):
            off = base + i * SC_WINDOW
            pltpu.sync_copy(idx_hbm.at[pl.ds(off, SC_WINDOW)], idx_v)
            cu = pltpu.async_copy(u_hbm.at[idx_v], ru_v, sem_u)
            cv = pltpu.async_copy(v_hbm.at[idx_v], rv_v, sem_v)
            cu.wait()
            pltpu.sync_copy(ru_v, ou_hbm.at[pl.ds(off, SC_WINDOW)])
            cv.wait()
            pltpu.sync_copy(rv_v, ov_hbm.at[pl.ds(off, SC_WINDOW)])

    return gather(table_u, table_v, idx)


def _peer_back_body(ug_ref, vg_ref, xn_ref, gate_ref, h2_ref, gf_ref, y_ref, olo_ref, ohi_ref, *, slots_per_step):
    sg = pl.program_id(1)
    tp = xn_ref.shape[0]
    xlo = xn_ref[:, :PACKED_WIDTH].astype(F32)
    xhi = xn_ref[:, PACKED_WIDTH:].astype(F32)
    gate = gate_ref[...]
    lane = lax.broadcasted_iota(I32, (tp, PEER_SLOTS), 1)

    @pl.when(sg == 0)
    def _():
        olo_ref[...] = jnp.zeros_like(olo_ref)
        ohi_ref[...] = jnp.zeros_like(ohi_ref)

    def slot(j, carry):
        olo, ohi = carry
        s = sg * slots_per_step + j
        ulo, uhi = _unpack(ug_ref[0, j])
        act = jnp.sum(ulo * xlo + uhi * xhi, axis=1, keepdims=True)
        g = jnp.sum(jnp.where(lane == s, gate, 0.0), axis=1, keepdims=True)
        w = g * jax.nn.gelu(act)
        vlo, vhi = _unpack(vg_ref[0, j])
        return olo + w * vlo, ohi + w * vhi

    olo, ohi = lax.fori_loop(0, slots_per_step, slot, (olo_ref[...], ohi_ref[...]))
    olo_ref[...] = olo
    ohi_ref[...] = ohi

    @pl.when(sg == pl.num_programs(1) - 1)
    def _():
        h3 = h2_ref[...] + jnp.concatenate([olo, ohi], axis=1)
        y_ref[...] = _rmsnorm(h3, gf_ref[...])


def _peer_back(ug, vg, xn, gate, h2, g_final, *, tp, slots_per_step):
    n = xn.shape[0]
    n_sg = PEER_SLOTS // slots_per_step
    rows = pl.BlockSpec((1, slots_per_step, tp, PACKED_WIDTH), lambda i, s: (i, s, 0, 0))
    tok = lambda w: pl.BlockSpec((tp, w), lambda i, s: (i, 0))
    return pl.pallas_call(
        functools.partial(_peer_back_body, slots_per_step=slots_per_step),
        grid=(n // tp, n_sg),
        in_specs=[rows, rows, tok(D_MODEL), tok(PEER_SLOTS), tok(D_MODEL), _full((1, D_MODEL))],
        out_specs=tok(D_MODEL),
        out_shape=jax.ShapeDtypeStruct((n, D_MODEL), F32),
        scratch_shapes=[pltpu.VMEM((tp, PACKED_WIDTH), F32), pltpu.VMEM((tp, PACKED_WIDTH), F32)],
        compiler_params=_cparams(("arbitrary", "arbitrary")), name="peer_back",
    )(ug, vg, xn, gate, h2, g_final.reshape(1, D_MODEL))


PROJ_TILE = 256
ATTN_TILE = 512
PEER_TILE = 256
PEER_SLOT_STEP = 16
PEER_CHUNK = 8192
SAMPLE_PAGES_PER_STEP = 8


def _post_attention(x2, attn, cbz, sga, sgb, mk, mv, n_batch, W, xa_tile):
    n = x2.shape[0]
    tm = min(PROJ_TILE, n)
    h1, qx = _merge(x2, attn, cbz, sga, sgb, W["wfo"], W["wco"], W["wo"], W["g_xattn"], W["wxq"], tm=tm)
    o = _cross_attn(qx.reshape(n_batch, n // n_batch, D_MODEL), mk, mv, tm=xa_tile).reshape(n, D_MODEL)
    h2, xn, eidx, gate = _peer_front(h1, o, W["wxo"], W["g_peer"], W["wpq"], W["ka"], W["kb"], tm=PEER_TILE)
    ys = []
    chunk = min(PEER_CHUNK, n)
    tiles = chunk // PEER_TILE
    for c in range(n // chunk):
        idx = eidx[c * tiles:(c + 1) * tiles].reshape(-1)
        ug, vg = _sc_gather2(W["pu"], W["pv"], idx)
        shape4 = (tiles, PEER_SLOTS, PEER_TILE, PACKED_WIDTH)
        sl = slice(c * chunk, (c + 1) * chunk)
        ys.append(_peer_back(ug.reshape(shape4), vg.reshape(shape4), xn[sl], gate[sl], h2[sl], W["g_final"],
                             tp=PEER_TILE, slots_per_step=PEER_SLOT_STEP))
    return jnp.concatenate(ys, axis=0) if len(ys) > 1 else ys[0]


def kernel(x_prompt, x_sample, cache_fox_k, cache_fox_v, cache_fox_logf, cache_mem_k, cache_mem_v, state_conv,
           page_table, mem_prompt, g_mix, w_in, b_forget, w_conv, w_fox_out, w_conv_out, w_o, g_xattn, g_mem,
           w_xq, w_xk, w_xv, w_xo, g_peer, w_peer_q, peer_keys_a, peer_keys_b, peer_u, peer_v, g_final):
    depth = g_mix.shape[0]
    assert depth == 1
    l = 0
    b, t, _ = x_prompt.shape
    bd, s, _ = x_sample.shape
    n_pool = cache_fox_k.shape[1]
    mem_len = mem_prompt.shape[1]

    in_w = _prep_in_weights(w_in[l])
    ka, kb = _prep_peer_keys(peer_keys_a[l], peer_keys_b[l])
    W = dict(wfo=w_fox_out[l].astype(BF16), wco=w_conv_out[l].astype(BF16), wo=w_o[l].astype(BF16),
             g_xattn=g_xattn[l], wxq=w_xq[l].astype(BF16), wxo=w_xo[l].astype(BF16), g_peer=g_peer[l],
             wpq=w_peer_q[l].astype(BF16), ka=ka, kb=kb, pu=_pack_table(peer_u[l]), pv=_pack_table(peer_v[l]),
             g_final=g_final)

    xp = x_prompt.reshape(b * t, D_MODEL)
    prev0 = jnp.zeros((b * SUBLANES, CONV_WIDTH), F32)
    q, kf, vf, kb16, vb16, logf, cbz, sga, sgb, utail = _in_proj(
        xp, g_mix[l], *in_w, b_forget[l], w_conv[l], prev0, n_seq=b, seq_len=t, tm=PROJ_TILE)
    frow = _cumsum_time(logf.reshape(b, t, FOX_HEADS).transpose(0, 2, 1))
    fcol = frow.transpose(0, 2, 1)
    r3 = lambda a: a.reshape(b, t, FOX_WIDTH)
    attn = _fox_prompt(r3(q), r3(kb16), r3(vb16), fcol, frow, tq=ATTN_TILE).reshape(b * t, FOX_WIDTH)
    mk, mv = _memory_kv(mem_prompt.reshape(b * mem_len, D_MODEL), g_mem[l], w_xk[l].astype(BF16),
                        w_xv[l].astype(BF16), tm=PROJ_TILE)
    mk3 = mk.reshape(b, mem_len, D_MODEL)
    mv3 = mv.reshape(b, mem_len, D_MODEL)
    y_prompt = _post_attention(xp, attn, cbz, sga, sgb, mk3, mv3, b, W, ATTN_TILE).reshape(b, t, D_MODEL)
    heads = lambda a, nb, nt: a.reshape(1, nb, nt, FOX_HEADS, FOX_HEAD_DIM)
    conv_state_p = utail.reshape(b, SUBLANES, CONV_WIDTH)[:, SUBLANES - (CONV_K - 1):][None]

    xs = x_sample.reshape(bd * s, D_MODEL)
    prev_s = jnp.concatenate([jnp.zeros((bd, SUBLANES - (CONV_K - 1), CONV_WIDTH), F32), state_conv[l]],
                             axis=1).reshape(bd * SUBLANES, CONV_WIDTH)
    qs, kfs, vfs, kbs, vbs, logfs, cbzs, sgas, sgbs, utails = _in_proj(
        xs, g_mix[l], *in_w, b_forget[l], w_conv[l], prev_s, n_seq=bd, seq_len=s, tm=bd * s)
    pad_page = lambda a: jnp.pad(a.reshape(bd, s, FOX_WIDTH), ((0, 0), (0, PAGE_SIZE - s), (0, 0)))
    lf3 = logfs.reshape(bd, s, FOX_HEADS)
    lfcol = lf3.reshape(bd, s * FOX_HEADS, 1)
    lfrow = jnp.pad(lf3.transpose(0, 2, 1), ((0, 0), (0, 0), (0, LANES - s)))
    attn_s = _fox_sample(
        page_table, qs.reshape(bd, s, FOX_WIDTH), pad_page(kbs), pad_page(vbs), lfcol, lfrow,
        cache_fox_k[l].reshape(n_pool, PAGE_SIZE, FOX_WIDTH), cache_fox_v[l].reshape(n_pool, PAGE_SIZE, FOX_WIDTH),
        cache_fox_logf[l].transpose(0, 2, 1), pages_per_step=SAMPLE_PAGES_PER_STEP).reshape(bd * s, FOX_WIDTH)
    cmk = cache_mem_k[l].reshape(bd, mem_len, D_MODEL)
    cmv = cache_mem_v[l].reshape(bd, mem_len, D_MODEL)
    y_sample = _post_attention(xs, attn_s, cbzs, sgas, sgbs, cmk, cmv, bd, W, s).reshape(bd, s, D_MODEL)
    conv_state_s = utails.reshape(bd, SUBLANES, CONV_WIDTH)[:, SUBLANES - (CONV_K - 1):][None]

    return (y_prompt, y_sample,
            heads(kf, b, t), heads(vf, b, t), logf.reshape(1, b, t, FOX_HEADS),
            mk.reshape(1, b, mem_len, XA_HEADS, XA_HEAD_DIM), mv.reshape(1, b, mem_len, XA_HEADS, XA_HEAD_DIM),
            conv_state_p,
            heads(kfs, bd, s), heads(vfs, bd, s), logfs.reshape(1, bd, s, FOX_HEADS), conv_state_s)
```

```python
import functools

import jax
import jax.numpy as jnp
from jax import lax
from jax.experimental import pallas as pl
from jax.experimental.pallas import tpu as pltpu
from jax.experimental.pallas import tpu_sc as plsc

F32 = jnp.float32
BF16 = jnp.bfloat16
I32 = jnp.int32

D_MODEL = 1024
FOX_HEADS = 8
FOX_HEAD_DIM = 64
FOX_WIDTH = FOX_HEADS * FOX_HEAD_DIM
CONV_WIDTH = D_MODEL // 2
CONV_K = 3
PAGE_SIZE = 128
XA_HEADS = 4
XA_HEAD_DIM = D_MODEL // XA_HEADS
PEER_HEADS = 8
PEER_N_KEYS = 128
PEER_HALF = 64
PEER_TOPK = 16
PEER_SLOTS = PEER_HEADS * PEER_TOPK
PEER_TILE = 128
RMS_EPS = 1e-6

LANES = 128
SUBLANES = 8
VMEM_LIMIT = 56 * 1024 * 1024
NEG_INF = float("-inf")


def _cparams(sem):
    return pltpu.CompilerParams(dimension_semantics=sem, vmem_limit_bytes=VMEM_LIMIT)


def _full(shape):
    return pl.BlockSpec(shape, lambda *_: (0,) * len(shape))


def _rmsnorm(x, g):
    return x * lax.rsqrt(jnp.mean(x * x, axis=-1, keepdims=True) + RMS_EPS) * g


def _log_sigmoid(x):
    return jnp.minimum(x, 0.0) - jnp.log1p(jnp.exp(-jnp.abs(x)))


def _in_proj_body(x_ref, g_ref, wqkv_ref, wf_ref, wc_ref, wg_ref, bf_ref, wconv_ref, prev_ref,
                  q_ref, kf_ref, vf_ref, kb_ref, vb_ref, logf_ref, cbz_ref, sga_ref, sgb_ref, utail_ref,
                  carry_ref, *, seq_rows):
    tm = x_ref.shape[0]
    xn = _rmsnorm(x_ref[...], g_ref[...]).astype(BF16)

    qkv = jnp.dot(xn, wqkv_ref[...], preferred_element_type=F32)
    k = qkv[:, FOX_WIDTH:2 * FOX_WIDTH]
    v = qkv[:, 2 * FOX_WIDTH:]
    q_ref[...] = (qkv[:, :FOX_WIDTH] * (FOX_HEAD_DIM ** -0.5)).astype(BF16)
    kf_ref[...] = k
    vf_ref[...] = v
    kb_ref[...] = k.astype(BF16)
    vb_ref[...] = v.astype(BF16)

    fl = jnp.dot(xn, wf_ref[...], preferred_element_type=F32)
    logf_ref[...] = _log_sigmoid(fl[:, :FOX_HEADS] + bf_ref[...])

    c3 = jnp.dot(xn, wc_ref[...], preferred_element_type=F32)
    u = c3[:, CONV_WIDTH:2 * CONV_WIDTH] * c3[:, :CONV_WIDTH]
    cb = c3[:, 2 * CONV_WIDTH:]
    rows = lax.broadcasted_iota(I32, (tm, CONV_WIDTH), 0)
    if seq_rows is None:
        @pl.when(pl.program_id(1) == 0)
        def _():
            carry_ref[...] = prev_ref[...]
        hist = carry_ref[...]
        u1 = jnp.where(rows == 0, hist[7:8], pltpu.roll(u, 1, 0))
        u2 = jnp.where(rows == 0, hist[6:7], jnp.where(rows == 1, hist[7:8], pltpu.roll(u, 2, 0)))
        carry_ref[...] = u[tm - SUBLANES:]
        utail_ref[...] = u[tm - SUBLANES:]
    else:
        t = rows % seq_rows
        prev = prev_ref[...]
        u1 = jnp.where(t == 0, pltpu.roll(prev, tm - (seq_rows - 1), 0), pltpu.roll(u, 1, 0))
        u2 = jnp.where(t < 2, pltpu.roll(prev, tm - (seq_rows - 2), 0), pltpu.roll(u, 2, 0))
        utail_ref[...] = u
    wconv = wconv_ref[...]
    conv = u2 * wconv[0:1] + u1 * wconv[1:2] + u * wconv[2:3]
    cbz_ref[...] = (cb * conv).astype(BF16)

    gates = jnp.dot(xn, wg_ref[...], preferred_element_type=F32)
    sga_ref[...] = jax.nn.sigmoid(gates[:, :D_MODEL])
    sgb_ref[...] = jax.nn.sigmoid(gates[:, D_MODEL:])


def _prep_in_weights(w_in):
    o_f = 3 * FOX_WIDTH
    o_c = o_f + FOX_HEADS
    o_g = o_c + 3 * CONV_WIDTH
    wqkv = w_in[:, :o_f].astype(BF16)
    wf = jnp.pad(w_in[:, o_f:o_c], ((0, 0), (0, LANES - FOX_HEADS))).astype(BF16)
    wc = w_in[:, o_c:o_g].astype(BF16)
    wg = w_in[:, o_g:].astype(BF16)
    return wqkv, wf, wc, wg


def _in_proj(x2, g_mix, wqkv, wf, wc, wg, b_forget, w_conv, prev8, *, n_seq, seq_len, tm):
    n = x2.shape[0]
    if seq_len >= tm:
        assert seq_len % tm == 0
        nt = seq_len // tm
        grid = (n_seq, nt)
        tok = lambda w: pl.BlockSpec((tm, w), lambda b, t: (b * nt + t, 0))
        seq8 = pl.BlockSpec((SUBLANES, CONV_WIDTH), lambda b, t: (b, 0))
        seq_rows = None
    else:
        assert seq_len == SUBLANES and tm == n
        grid = (1, 1)
        tok = lambda w: pl.BlockSpec((tm, w), lambda b, t: (0, 0))
        seq8 = pl.BlockSpec((tm, CONV_WIDTH), lambda b, t: (0, 0))
        seq_rows = seq_len
    out_shape = (
        jax.ShapeDtypeStruct((n, FOX_WIDTH), BF16),
        jax.ShapeDtypeStruct((n, FOX_WIDTH), F32),
        jax.ShapeDtypeStruct((n, FOX_WIDTH), F32),
        jax.ShapeDtypeStruct((n, FOX_WIDTH), BF16),
        jax.ShapeDtypeStruct((n, FOX_WIDTH), BF16),
        jax.ShapeDtypeStruct((n, FOX_HEADS), F32),
        jax.ShapeDtypeStruct((n, CONV_WIDTH), BF16),
        jax.ShapeDtypeStruct((n, D_MODEL), F32),
        jax.ShapeDtypeStruct((n, D_MODEL), F32),
        jax.ShapeDtypeStruct((n_seq * SUBLANES, CONV_WIDTH), F32),
    )
    out_specs = (tok(FOX_WIDTH), tok(FOX_WIDTH), tok(FOX_WIDTH), tok(FOX_WIDTH), tok(FOX_WIDTH),
                 tok(FOX_HEADS), tok(CONV_WIDTH), tok(D_MODEL), tok(D_MODEL), seq8)
    in_specs = [tok(D_MODEL), _full((1, D_MODEL)), _full(wqkv.shape), _full(wf.shape), _full(wc.shape),
                _full(wg.shape), _full((1, FOX_HEADS)), _full((CONV_K, CONV_WIDTH)), seq8]
    return pl.pallas_call(
        functools.partial(_in_proj_body, seq_rows=seq_rows),
        grid=grid, in_specs=in_specs, out_specs=out_specs, out_shape=out_shape,
        scratch_shapes=[pltpu.VMEM((SUBLANES, CONV_WIDTH), F32)],
        compiler_params=_cparams(("arbitrary", "arbitrary")),
        name="in_proj",
    )(x2, g_mix.reshape(1, D_MODEL), wqkv, wf, wc, wg, b_forget.reshape(1, FOX_HEADS), w_conv, prev8)


def _lane_prefix_scan(x, lane):
    s = x
    sh = 1
    while sh < LANES:
        s = s + jnp.where(lane >= sh, pltpu.roll(s, sh, 1), 0.0)
        sh *= 2
    return s


def _cumsum_body(x_ref, o_ref):
    n_chunks = x_ref.shape[2] // LANES
    lane = lax.broadcasted_iota(I32, (FOX_HEADS, LANES), 1)

    def step(c, carry):
        off = pl.multiple_of(c * LANES, LANES)
        s = _lane_prefix_scan(x_ref[0, :, pl.ds(off, LANES)], lane) + carry
        o_ref[0, :, pl.ds(off, LANES)] = s
        return s[:, LANES - 1:]

    lax.fori_loop(0, n_chunks, step, jnp.zeros((FOX_HEADS, 1), F32))


def _cumsum_time(logf_t):
    b, h, t = logf_t.shape
    spec = pl.BlockSpec((1, h, t), lambda i: (i, 0, 0))
    return pl.pallas_call(
        _cumsum_body, grid=(b,), in_specs=[spec], out_specs=spec,
        out_shape=jax.ShapeDtypeStruct(logf_t.shape, F32),
        compiler_params=_cparams(("arbitrary",)), name="forget_cumsum",
    )(logf_t)


def _fox_prompt_body(q_ref, k_ref, v_ref, fcol_ref, frow_ref, o_ref, *, tq):
    hp = pl.program_id(1)
    qi = pl.program_id(2)
    q2 = q_ref[0]
    lane = lax.broadcasted_iota(I32, (tq, LANES), 1)
    upper = lane >= FOX_HEAD_DIM
    col8 = lax.broadcasted_iota(I32, (tq, FOX_HEADS), 1)
    fcol8 = fcol_ref[0]
    rowi = lax.broadcasted_iota(I32, (tq, tq), 0)
    coli = lax.broadcasted_iota(I32, (tq, tq), 1)
    zero = jnp.zeros_like(q2)

    outs = []
    for hh in range(2):
        h = 2 * hp + hh
        qh = jnp.where(upper, q2, zero) if hh else jnp.where(upper, zero, q2)
        fq = jnp.sum(jnp.where(col8 == h, fcol8, 0.0), axis=1, keepdims=True)

        def block(kb, carry, masked):
            m, l, acc = carry
            off = pl.multiple_of(kb * tq, tq)
            kk = k_ref[0, pl.ds(off, tq), :]
            vv = v_ref[0, pl.ds(off, tq), :]
            fk = frow_ref[0, pl.ds(h, 1), pl.ds(off, tq)]
            s = lax.dot_general(qh, kk, (((1,), (1,)), ((), ())), preferred_element_type=F32)
            s = s + fq - fk
            if masked:
                s = jnp.where(coli <= rowi, s, NEG_INF)
            m_new = jnp.maximum(m, jnp.max(s, axis=1, keepdims=True))
            p = jnp.exp(s - m_new)
            alpha = jnp.exp(m - m_new)
            l = alpha * l + jnp.sum(p, axis=1, keepdims=True)
            acc = alpha * acc + jnp.dot(p.astype(BF16), vv, preferred_element_type=F32)
            return m_new, l, acc

        init = (jnp.full((tq, 1), NEG_INF, F32), jnp.zeros((tq, 1), F32), jnp.zeros((tq, LANES), F32))
        carry = lax.fori_loop(0, qi, functools.partial(block, masked=False), init)
        m, l, acc = block(qi, carry, True)
        outs.append(acc / l)
    o_ref[0] = jnp.where(upper, outs[1], outs[0]).astype(o_ref.dtype)


def _fox_prompt(q, k, v, fcol, frow, *, tq):
    b, t, _ = q.shape
    n_pairs = FOX_WIDTH // LANES
    return pl.pallas_call(
        functools.partial(_fox_prompt_body, tq=tq),
        grid=(b, n_pairs, t // tq),
        in_specs=[
            pl.BlockSpec((1, tq, LANES), lambda i, p, j: (i, j, p)),
            pl.BlockSpec((1, t, LANES), lambda i, p, j: (i, 0, p)),
            pl.BlockSpec((1, t, LANES), lambda i, p, j: (i, 0, p)),
            pl.BlockSpec((1, tq, FOX_HEADS), lambda i, p, j: (i, j, 0)),
            pl.BlockSpec((1, FOX_HEADS, t), lambda i, p, j: (i, 0, 0)),
        ],
        out_specs=pl.BlockSpec((1, tq, LANES), lambda i, p, j: (i, j, p)),
        out_shape=jax.ShapeDtypeStruct((b, t, FOX_WIDTH), BF16),
        compiler_params=_cparams(("arbitrary", "arbitrary", "arbitrary")),
        name="fox_prompt",
    )(q, k, v, fcol, frow)


SAMPLE_ROWS = 64
PAGE_ROWS = PAGE_SIZE * FOX_HEADS


def _flat_suffix_excl(x, lane, sub):
    e = jnp.where(lane < LANES - FOX_HEADS, pltpu.roll(x, LANES - FOX_HEADS, 1), 0.0)
    sh = FOX_HEADS
    while sh < LANES:
        e = e + jnp.where(lane < LANES - sh, pltpu.roll(e, LANES - sh, 1), 0.0)
        sh *= 2
    tt = jnp.where(lane < FOX_HEADS, e + x, 0.0)
    sh = FOX_HEADS
    while sh < LANES:
        tt = tt + pltpu.roll(tt, sh, 1)
        sh *= 2
    c = jnp.where(sub < SUBLANES - 1, pltpu.roll(tt, SUBLANES - 1, 0), 0.0)
    sh = 1
    while sh < SUBLANES:
        c = c + jnp.where(sub < SUBLANES - sh, pltpu.roll(c, SUBLANES - sh, 0), 0.0)
        sh *= 2
    return e + c, c[0:1] + tt[0:1]


def _fox_sample2_body(pt_ref, q_ref, kn_ref, vn_ref, lfcol_ref, lfnew_ref, *rest, pages_per_step):
    pg = pages_per_step
    k_refs = rest[:pg]
    v_refs = rest[pg:2 * pg]
    lf_refs = rest[2 * pg:3 * pg]
    o_ref = rest[3 * pg]
    m_ref, l_ref, acc_ref, tot_ref, rt_ref = rest[3 * pg + 1:]
    g = pl.program_id(1)
    lane = lax.broadcasted_iota(I32, (SUBLANES, LANES), 1)
    sub = lax.broadcasted_iota(I32, (SUBLANES, LANES), 0)
    rr = lax.broadcasted_iota(I32, (SAMPLE_ROWS, PAGE_ROWS), 0)
    cc = lax.broadcasted_iota(I32, (SAMPLE_ROWS, PAGE_ROWS), 1)
    same_head = (rr % FOX_HEADS) == (cc % FOX_HEADS)
    nt = (((1,), (1,)), ((), ()))

    def scores(k2, lfa, tot, visible):
        suffix, page_tot = _flat_suffix_excl(lfa, lane, sub)
        r = suffix + tot
        bias = jnp.concatenate([r[i:i + 1] for i in range(SUBLANES)], axis=1)
        s = lax.dot_general(q_ref[0], k2, nt, preferred_element_type=F32)
        return jnp.where(visible, s + rt_ref[...] + bias, NEG_INF), tot + page_tot

    @pl.when(g == 0)
    def _():
        r1 = lax.broadcasted_iota(I32, (SAMPLE_ROWS, 1), 0)
        x = lfcol_ref[0]
        s = jnp.where(r1 < SAMPLE_ROWS - 8, pltpu.roll(x, SAMPLE_ROWS - 8, 0), 0.0)
        for sh in (8, 16, 32):
            s = s + jnp.where(r1 < SAMPLE_ROWS - sh, pltpu.roll(s, SAMPLE_ROWS - sh, 0), 0.0)
        rt_ref[...] = -s
        causal = same_head & (cc // FOX_HEADS <= rr // FOX_HEADS)
        s, tot0 = scores(kn_ref[0], lfnew_ref[0], jnp.zeros((1, LANES), F32), causal)
        m0 = jnp.max(s, axis=1, keepdims=True)
        p = jnp.exp(s - m0)
        m_ref[...] = m0
        l_ref[...] = jnp.sum(p, axis=1, keepdims=True)
        acc_ref[...] = jnp.dot(p.astype(BF16), vn_ref[0], preferred_element_type=F32)
        tot_ref[...] = tot0

    tot = tot_ref[...]
    ss = []
    for i in reversed(range(pg)):
        k2 = k_refs[i][0].reshape(PAGE_ROWS, FOX_HEAD_DIM).astype(BF16)
        s, tot = scores(k2, lf_refs[i][0], tot, same_head)
        ss.append(s)
    tot_ref[...] = tot
    m_old = m_ref[...]
    m_new = m_old
    for s in ss:
        m_new = jnp.maximum(m_new, jnp.max(s, axis=1, keepdims=True))
    alpha = jnp.exp(m_old - m_new)
    l_new = alpha * l_ref[...]
    acc = alpha * acc_ref[...]
    for s, i in zip(ss, reversed(range(pg))):
        p = jnp.exp(s - m_new)
        l_new = l_new + jnp.sum(p, axis=1, keepdims=True)
        v2 = v_refs[i][0].reshape(PAGE_ROWS, FOX_HEAD_DIM).astype(BF16)
        acc = acc + jnp.dot(p.astype(BF16), v2, preferred_element_type=F32)
    m_ref[...] = m_new
    l_ref[...] = l_new
    acc_ref[...] = acc

    @pl.when(g == pl.num_programs(1) - 1)
    def _():
        o_ref[0] = (acc_ref[...] / l_ref[...]).astype(o_ref.dtype)


def _fox_sample2(page_table, q2, kn2, vn2, lfcol, lfnew, cache_k, cache_v, cache_lf, *, pages_per_step):
    bd, n_pages = page_table.shape
    pg = pages_per_step
    assert n_pages % pg == 0
    ng = n_pages // pg

    def page_spec(shape, i):
        zeros = (0,) * (len(shape) - 1)
        return pl.BlockSpec(shape, lambda b, g, pt: (pt[b, (ng - 1 - g) * pg + i],) + zeros)

    req = lambda shape: pl.BlockSpec(shape, lambda b, g, pt: (b, 0, 0))
    in_specs = [req((1, SAMPLE_ROWS, FOX_HEAD_DIM)), req((1, PAGE_ROWS, FOX_HEAD_DIM)),
                req((1, PAGE_ROWS, FOX_HEAD_DIM)), req((1, SAMPLE_ROWS, 1)), req((1, SUBLANES, LANES))]
    in_specs += [page_spec((1, PAGE_SIZE, FOX_HEADS, FOX_HEAD_DIM), i) for i in range(pg)]
    in_specs += [page_spec((1, PAGE_SIZE, FOX_HEADS, FOX_HEAD_DIM), i) for i in range(pg)]
    in_specs += [page_spec((1, SUBLANES, LANES), i) for i in range(pg)]
    grid_spec = pltpu.PrefetchScalarGridSpec(
        num_scalar_prefetch=1, grid=(bd, ng), in_specs=in_specs,
        out_specs=req((1, SAMPLE_ROWS, FOX_HEAD_DIM)),
        scratch_shapes=[pltpu.VMEM((SAMPLE_ROWS, 1), F32), pltpu.VMEM((SAMPLE_ROWS, 1), F32),
                        pltpu.VMEM((SAMPLE_ROWS, FOX_HEAD_DIM), F32), pltpu.VMEM((1, LANES), F32),
                        pltpu.VMEM((SAMPLE_ROWS, 1), F32)])
    return pl.pallas_call(
        functools.partial(_fox_sample2_body, pages_per_step=pg),
        grid_spec=grid_spec, out_shape=jax.ShapeDtypeStruct((bd, SAMPLE_ROWS, FOX_HEAD_DIM), BF16),
        compiler_params=_cparams(("arbitrary", "arbitrary")), name="fox_sample",
    )(page_table, q2, kn2, vn2, lfcol, lfnew, *([cache_k] * pg), *([cache_v] * pg), *([cache_lf] * pg))


def _merge_body(x_ref, attn_ref, cbz_ref, sga_ref, sgb_ref, wfo_ref, wco_ref, wo_ref, gx_ref, wxq_ref,
                h1_ref, qx_ref):
    ya = jnp.dot(attn_ref[...], wfo_ref[...], preferred_element_type=F32)
    yb = jnp.dot(cbz_ref[...], wco_ref[...], preferred_element_type=F32)
    mix = (sga_ref[...] * ya + sgb_ref[...] * yb).astype(BF16)
    h1 = x_ref[...] + jnp.dot(mix, wo_ref[...], preferred_element_type=F32)
    h1_ref[...] = h1
    xn = _rmsnorm(h1, gx_ref[...]).astype(BF16)
    qx_ref[...] = (jnp.dot(xn, wxq_ref[...], preferred_element_type=F32) * (XA_HEAD_DIM ** -0.5)).astype(BF16)


def _merge(x2, attn, cbz, sga, sgb, wfo, wco, wo, g_xattn, wxq, *, tm):
    n = x2.shape[0]
    tok = lambda w: pl.BlockSpec((tm, w), lambda i: (i, 0))
    return pl.pallas_call(
        _merge_body, grid=(n // tm,),
        in_specs=[tok(D_MODEL), tok(FOX_WIDTH), tok(CONV_WIDTH), tok(D_MODEL), tok(D_MODEL),
                  _full(wfo.shape), _full(wco.shape), _full(wo.shape), _full((1, D_MODEL)), _full(wxq.shape)],
        out_specs=(tok(D_MODEL), tok(D_MODEL)),
        out_shape=(jax.ShapeDtypeStruct((n, D_MODEL), F32), jax.ShapeDtypeStruct((n, D_MODEL), BF16)),
        compiler_params=_cparams(("arbitrary",)), name="merge",
    )(x2, attn, cbz, sga, sgb, wfo, wco, wo, g_xattn.reshape(1, D_MODEL), wxq)


def _memory_kv_body(m_ref, g_ref, wk_ref, wv_ref, mk_ref, mv_ref):
    mn = _rmsnorm(m_ref[...], g_ref[...]).astype(BF16)
    mk_ref[...] = jnp.dot(mn, wk_ref[...], preferred_element_type=F32)
    mv_ref[...] = jnp.dot(mn, wv_ref[...], preferred_element_type=F32)


def _memory_kv(mem2, g_mem, wxk, wxv, *, tm):
    n = mem2.shape[0]
    tok = pl.BlockSpec((tm, D_MODEL), lambda i: (i, 0))
    return pl.pallas_call(
        _memory_kv_body, grid=(n // tm,),
        in_specs=[tok, _full((1, D_MODEL)), _full(wxk.shape), _full(wxv.shape)],
        out_specs=(tok, tok),
        out_shape=(jax.ShapeDtypeStruct((n, D_MODEL), F32),) * 2,
        compiler_params=_cparams(("arbitrary",)), name="memory_kv",
    )(mem2, g_mem.reshape(1, D_MODEL), wxk, wxv)


def _cross_attn_body(q_ref, mk_ref, mv_ref, o_ref):
    q = q_ref[0]
    outs = []
    for h in range(XA_HEADS):
        sl = slice(h * XA_HEAD_DIM, (h + 1) * XA_HEAD_DIM)
        kh = mk_ref[0, :, sl].astype(BF16)
        vh = mv_ref[0, :, sl].astype(BF16)
        s = lax.dot_general(q[:, sl], kh, (((1,), (1,)), ((), ())), preferred_element_type=F32)
        p = jnp.exp(s - jnp.max(s, axis=1, keepdims=True))
        p = p / jnp.sum(p, axis=1, keepdims=True)
        outs.append(jnp.dot(p.astype(BF16), vh, preferred_element_type=F32))
    o_ref[0] = jnp.concatenate(outs, axis=1).astype(o_ref.dtype)


def _cross_attn(qx, mk, mv, *, tm):
    b, t, _ = qx.shape
    m = mk.shape[1]
    tok = pl.BlockSpec((1, tm, D_MODEL), lambda i, j: (i, j, 0))
    mem = pl.BlockSpec((1, m, D_MODEL), lambda i, j: (i, 0, 0))
    return pl.pallas_call(
        _cross_attn_body, grid=(b, t // tm), in_specs=[tok, mem, mem], out_specs=tok,
        out_shape=jax.ShapeDtypeStruct((b, t, D_MODEL), BF16),
        compiler_params=_cparams(("arbitrary", "arbitrary")), name="cross_attn",
    )(qx, mk, mv)


PEER_PAIRS = tuple((i, j) for i in range(PEER_TOPK) for j in range(PEER_TOPK) if (i + 1) * (j + 1) <= PEER_TOPK)
PEER_CAND_ROWS = -(-len(PEER_PAIRS) // SUBLANES) * SUBLANES


def _top_rows(s, k, payload=None):
    n = s.shape[0]
    iota = lax.broadcasted_iota(I32, s.shape, 0)
    vals, ids = [], []
    for _ in range(k):
        m = jnp.max(s, axis=0, keepdims=True)
        r = jnp.min(jnp.where(s == m, iota, n), axis=0, keepdims=True)
        hit = iota == r
        vals.append(m)
        ids.append(r if payload is None else jnp.max(jnp.where(hit, payload, -1), axis=0, keepdims=True))
        s = jnp.where(hit, NEG_INF, s)
    return vals, ids


def _peer_front_body(h1_ref, o_ref, wxo_ref, gp_ref, wpq_ref, ka_ref, kb_ref,
                     h2_ref, xn_ref, eidx_ref, gate_ref, qp_ref, e_ref, g_ref):
    tm = h1_ref.shape[0]
    h2 = h1_ref[...] + jnp.dot(o_ref[...], wxo_ref[...], preferred_element_type=F32)
    h2_ref[...] = h2
    xn = _rmsnorm(h2, gp_ref[...]).astype(BF16)
    xn_ref[...] = xn
    qp_ref[...] = jnp.dot(xn, wpq_ref[...], preferred_element_type=F32).astype(BF16)

    def head(h, _):
        qh = qp_ref[:, pl.ds(pl.multiple_of(h * LANES, LANES), LANES)]
        nt = (((1,), (1,)), ((), ()))
        sa = lax.dot_general(ka_ref[h], qh, nt, preferred_element_type=F32)
        sb = lax.dot_general(kb_ref[h], qh, nt, preferred_element_type=F32)
        va, ia = _top_rows(sa, PEER_TOPK)
        vb, ib = _top_rows(sb, PEER_TOPK)
        pad = PEER_CAND_ROWS - len(PEER_PAIRS)
        cand = jnp.concatenate([va[i] + vb[j] for i, j in PEER_PAIRS]
                               + [jnp.full((pad, tm), NEG_INF, F32)], axis=0)
        ceid = jnp.concatenate([ia[i] * PEER_N_KEYS + ib[j] for i, j in PEER_PAIRS]
                               + [jnp.zeros((pad, tm), I32)], axis=0)
        ts, te = _top_rows(cand, PEER_TOPK, payload=ceid)
        ex = [jnp.exp(t - ts[0]) for t in ts]
        den = ex[0]
        for e in ex[1:]:
            den = den + e
        row0 = pl.multiple_of(h * PEER_TOPK, PEER_TOPK)
        e_ref[pl.ds(row0, PEER_TOPK), :] = jnp.concatenate(te, axis=0)
        g_ref[pl.ds(row0, PEER_TOPK), :] = jnp.concatenate([e / den for e in ex], axis=0)
        return 0

    lax.fori_loop(0, PEER_HEADS, head, 0)
    for c in range(eidx_ref.shape[0]):
        eidx_ref[c] = e_ref[:, c * PEER_TILE:(c + 1) * PEER_TILE]
    gate_ref[...] = g_ref[...].T


def _prep_peer_keys(keys_a, keys_b):
    z = jnp.zeros_like(keys_a)
    return (jnp.concatenate([keys_a, z], axis=-1).astype(BF16), jnp.concatenate([z, keys_b], axis=-1).astype(BF16))


def _peer_front(h1, o, wxo, g_peer, wpq, ka_pad, kb_pad, *, tm):
    n = h1.shape[0]
    assert tm % PEER_TILE == 0
    sub = tm // PEER_TILE
    tok = lambda w: pl.BlockSpec((tm, w), lambda i: (i, 0))
    return pl.pallas_call(
        _peer_front_body, grid=(n // tm,),
        in_specs=[tok(D_MODEL), tok(D_MODEL), _full(wxo.shape), _full((1, D_MODEL)), _full(wpq.shape),
                  _full(ka_pad.shape), _full(kb_pad.shape)],
        out_specs=(tok(D_MODEL), tok(D_MODEL), pl.BlockSpec((sub, PEER_SLOTS, PEER_TILE), lambda i: (i, 0, 0)),
                   tok(PEER_SLOTS)),
        out_shape=(jax.ShapeDtypeStruct((n, D_MODEL), F32), jax.ShapeDtypeStruct((n, D_MODEL), BF16),
                   jax.ShapeDtypeStruct((n // PEER_TILE, PEER_SLOTS, PEER_TILE), I32),
                   jax.ShapeDtypeStruct((n, PEER_SLOTS), F32)),
        scratch_shapes=[pltpu.VMEM((tm, PEER_HEADS * LANES), BF16), pltpu.VMEM((PEER_SLOTS, tm), I32),
                        pltpu.VMEM((PEER_SLOTS, tm), F32)],
        compiler_params=_cparams(("arbitrary",)), name="peer_front",
    )(h1, o, wxo, g_peer.reshape(1, D_MODEL), wpq, ka_pad, kb_pad)


PACKED_WIDTH = D_MODEL // 2


def _pack_table(t):
    b = lax.bitcast_convert_type(t.astype(BF16), jnp.uint16).astype(jnp.uint32)
    return lax.bitcast_convert_type(b[:, :PACKED_WIDTH] | (b[:, PACKED_WIDTH:] << 16), I32)


def _unpack(w):
    lo = lax.bitcast_convert_type(w << 16, F32)
    hi = lax.bitcast_convert_type(w & jnp.int32(-65536), F32)
    return lo, hi


SC_CORES = 2
SC_SUBCORES = 16
SC_WINDOW = 64


def _sc_gather2(table_u, table_v, idx):
    r = idx.shape[0]
    workers = SC_CORES * SC_SUBCORES
    assert r % (workers * SC_WINDOW) == 0
    per_worker = r // workers
    n_win = per_worker // SC_WINDOW
    width = table_u.shape[1]
    mesh = plsc.VectorSubcoreMesh(core_axis_name="c", subcore_axis_name="s")
    out = jax.ShapeDtypeStruct((r, width), table_u.dtype)

    @functools.partial(
        pl.kernel, mesh=mesh, out_type=(out, out),
        scratch_types=[pltpu.VMEM((SC_WINDOW,), I32), pltpu.VMEM((SC_WINDOW, width), table_u.dtype),
                       pltpu.VMEM((SC_WINDOW, width), table_u.dtype),
                       pltpu.SemaphoreType.DMA, pltpu.SemaphoreType.DMA])
    def gather(u_hbm, v_hbm, idx_hbm, ou_hbm, ov_hbm, idx_v, ru_v, rv_v, sem_u, sem_v):
        wid = lax.axis_index("s") * SC_CORES + lax.axis_index("c")
        base = wid * per_worker

        @pl.loop(0, n_win)
        def _(i):
            off = base + i * SC_WINDOW
            pltpu.sync_copy(idx_hbm.at[pl.ds(off, SC_WINDOW)], idx_v)
            cu = pltpu.async_copy(u_hbm.at[idx_v], ru_v, sem_u)
            cv = pltpu.async_copy(v_hbm.at[idx_v], rv_v, sem_v)
            cu.wait()
            pltpu.sync_copy(ru_v, ou_hbm.at[pl.ds(off, SC_WINDOW)])
            cv.wait()
            pltpu.sync_copy(rv_v, ov_hbm.at[pl.ds(off, SC_WINDOW)])

    return gather(table_u, table_v, idx)


def _peer_back_body(ug_ref, vg_ref, xn_ref, gate_ref, h2_ref, gf_ref, y_ref, x32_ref, out_ref, *, slots_per_step):
    sg = pl.program_id(1)
    tp = xn_ref.shape[0]
    n_groups = tp // SUBLANES
    lane = lax.broadcasted_iota(I32, (SUBLANES, PEER_SLOTS), 1)
    slot0 = sg * slots_per_step

    @pl.when(sg == 0)
    def _():
        x32_ref[...] = xn_ref[...].astype(F32)
        out_ref[...] = jnp.zeros_like(out_ref)

    def rows_of(r):
        return pl.ds(pl.multiple_of(r * SUBLANES, SUBLANES), SUBLANES)

    def u_pass(r):
        rows = rows_of(r)
        xlo = x32_ref[rows, :PACKED_WIDTH]
        xhi = x32_ref[rows, PACKED_WIDTH:]
        act = jnp.zeros((SUBLANES, PEER_SLOTS), F32)
        for j in range(slots_per_step):
            ulo, uhi = _unpack(ug_ref[0, j, rows, :])
            a = jnp.sum(ulo * xlo + uhi * xhi, axis=1, keepdims=True)
            act = jnp.where(lane == slot0 + j, a, act)
        return gate_ref[rows, :] * jax.nn.gelu(act)

    def v_pass(r, wts):
        rows = rows_of(r)
        olo = out_ref[rows, :PACKED_WIDTH]
        ohi = out_ref[rows, PACKED_WIDTH:]
        for j in range(slots_per_step):
            w = jnp.sum(jnp.where(lane == slot0 + j, wts, 0.0), axis=1, keepdims=True)
            vlo, vhi = _unpack(vg_ref[0, j, rows, :])
            olo = olo + w * vlo
            ohi = ohi + w * vhi
        out_ref[rows, :PACKED_WIDTH] = olo
        out_ref[rows, PACKED_WIDTH:] = ohi

    def group(r, wts):
        nxt = u_pass(r)
        v_pass(r - 1, wts)
        return nxt

    last = lax.fori_loop(1, n_groups, group, u_pass(0))
    v_pass(n_groups - 1, last)

    @pl.when(sg == pl.num_programs(1) - 1)
    def _():
        y_ref[...] = _rmsnorm(h2_ref[...] + out_ref[...], gf_ref[...])


def _peer_back(ug, vg, xn, gate, h2, g_final, *, tp, slots_per_step):
    n = xn.shape[0]
    n_sg = PEER_SLOTS // slots_per_step
    rows = pl.BlockSpec((1, slots_per_step, tp, PACKED_WIDTH), lambda i, s: (i, s, 0, 0))
    tok = lambda w: pl.BlockSpec((tp, w), lambda i, s: (i, 0))
    return pl.pallas_call(
        functools.partial(_peer_back_body, slots_per_step=slots_per_step),
        grid=(n // tp, n_sg),
        in_specs=[rows, rows, tok(D_MODEL), tok(PEER_SLOTS), tok(D_MODEL), _full((1, D_MODEL))],
        out_specs=tok(D_MODEL),
        out_shape=jax.ShapeDtypeStruct((n, D_MODEL), F32),
        scratch_shapes=[pltpu.VMEM((tp, D_MODEL), F32), pltpu.VMEM((tp, D_MODEL), F32)],
        compiler_params=_cparams(("arbitrary", "arbitrary")), name="peer_back",
    )(ug, vg, xn, gate, h2, g_final.reshape(1, D_MODEL))


PROJ_TILE = 256
ATTN_TILE = 512
PEER_FRONT_TILE = 256
PEER_SLOT_STEP = 32
SAMPLE_PAGES_PER_STEP = 8


def _retrieve_start(x2, attn, cbz, sga, sgb, mk, mv, n_batch, W, xa_tile):
    n = x2.shape[0]
    h1, qx = _merge(x2, attn, cbz, sga, sgb, W["wfo"], W["wco"], W["wo"], W["g_xattn"], W["wxq"], tm=PROJ_TILE)
    o = _cross_attn(qx.reshape(n_batch, n // n_batch, D_MODEL), mk, mv, tm=xa_tile).reshape(n, D_MODEL)
    h2, xn, eidx, gate = _peer_front(h1, o, W["wxo"], W["g_peer"], W["wpq"], W["ka"], W["kb"], tm=PEER_FRONT_TILE)
    ug, vg = _sc_gather2(W["pu"], W["pv"], eidx.reshape(-1))
    return ug, vg, xn, gate, h2


def _retrieve_finish(pending, W):
    ug, vg, xn, gate, h2 = pending
    shape4 = (xn.shape[0] // PEER_TILE, PEER_SLOTS, PEER_TILE, PACKED_WIDTH)
    return _peer_back(ug.reshape(shape4), vg.reshape(shape4), xn, gate, h2, W["g_final"],
                      tp=PEER_TILE, slots_per_step=PEER_SLOT_STEP)


def kernel(x_prompt, x_sample, cache_fox_k, cache_fox_v, cache_fox_logf, cache_mem_k, cache_mem_v, state_conv,
           page_table, mem_prompt, g_mix, w_in, b_forget, w_conv, w_fox_out, w_conv_out, w_o, g_xattn, g_mem,
           w_xq, w_xk, w_xv, w_xo, g_peer, w_peer_q, peer_keys_a, peer_keys_b, peer_u, peer_v, g_final):
    depth = g_mix.shape[0]
    assert depth == 1
    l = 0
    b, t, _ = x_prompt.shape
    bd, s, _ = x_sample.shape
    n_pool = cache_fox_k.shape[1]
    mem_len = mem_prompt.shape[1]

    in_w = _prep_in_weights(w_in[l])
    ka, kb = _prep_peer_keys(peer_keys_a[l], peer_keys_b[l])
    W = dict(wfo=w_fox_out[l].astype(BF16), wco=w_conv_out[l].astype(BF16), wo=w_o[l].astype(BF16),
             g_xattn=g_xattn[l], wxq=w_xq[l].astype(BF16), wxo=w_xo[l].astype(BF16), g_peer=g_peer[l],
             wpq=w_peer_q[l].astype(BF16), ka=ka, kb=kb, pu=_pack_table(peer_u[l]), pv=_pack_table(peer_v[l]),
             g_final=g_final)

    mk, mv = _memory_kv(mem_prompt.reshape(b * mem_len, D_MODEL), g_mem[l], w_xk[l].astype(BF16),
                        w_xv[l].astype(BF16), tm=PROJ_TILE)
    mk3 = mk.reshape(b, mem_len, D_MODEL)
    mv3 = mv.reshape(b, mem_len, D_MODEL)
    prev0 = jnp.zeros((SUBLANES, CONV_WIDTH), F32)
    kf_rows, vf_rows, logf_rows, utail_rows, y_rows = [], [], [], [], []
    pending = None
    for r in range(b):
        q, kf, vf, kb16, vb16, logf, cbz, sga, sgb, utail = _in_proj(
            x_prompt[r], g_mix[l], *in_w, b_forget[l], w_conv[l], prev0, n_seq=1, seq_len=t, tm=PROJ_TILE)
        frow = _cumsum_time(logf.T[None])
        attn = _fox_prompt(q[None], kb16[None], vb16[None], frow.transpose(0, 2, 1), frow, tq=ATTN_TILE)[0]
        started = _retrieve_start(x_prompt[r], attn, cbz, sga, sgb, mk3[r:r + 1], mv3[r:r + 1], 1, W, ATTN_TILE)
        if pending is not None:
            y_rows.append(_retrieve_finish(pending, W))
        pending = started
        kf_rows.append(kf)
        vf_rows.append(vf)
        logf_rows.append(logf)
        utail_rows.append(utail)
    heads = lambda a, nb, nt: a.reshape(1, nb, nt, FOX_HEADS, FOX_HEAD_DIM)
    conv_state_p = jnp.stack(utail_rows)[:, SUBLANES - (CONV_K - 1):][None]

    xs = x_sample.reshape(bd * s, D_MODEL)
    prev_s = jnp.concatenate([jnp.zeros((bd, SUBLANES - (CONV_K - 1), CONV_WIDTH), F32), state_conv[l]],
                             axis=1).reshape(bd * SUBLANES, CONV_WIDTH)
    qs, kfs, vfs, kbs, vbs, logfs, cbzs, sgas, sgbs, utails = _in_proj(
        xs, g_mix[l], *in_w, b_forget[l], w_conv[l], prev_s, n_seq=bd, seq_len=s, tm=bd * s)
    assert s * FOX_HEADS == SAMPLE_ROWS
    rows2 = lambda a: a.reshape(bd, SAMPLE_ROWS, FOX_HEAD_DIM)
    pad_page = lambda a: jnp.pad(rows2(a), ((0, 0), (0, PAGE_ROWS - SAMPLE_ROWS), (0, 0)))
    lfcol = logfs.reshape(bd, SAMPLE_ROWS, 1)
    lfnew = jnp.pad(logfs.reshape(bd, 1, SAMPLE_ROWS), ((0, 0), (0, SUBLANES - 1), (0, LANES - SAMPLE_ROWS)))
    attn_s = _fox_sample2(
        page_table, rows2(qs), pad_page(kbs), pad_page(vbs), lfcol, lfnew, cache_fox_k[l], cache_fox_v[l],
        cache_fox_logf[l].reshape(n_pool, SUBLANES, LANES),
        pages_per_step=SAMPLE_PAGES_PER_STEP).reshape(bd * s, FOX_WIDTH)
    cmk = cache_mem_k[l].reshape(bd, mem_len, D_MODEL)
    cmv = cache_mem_v[l].reshape(bd, mem_len, D_MODEL)
    started_s = _retrieve_start(xs, attn_s, cbzs, sgas, sgbs, cmk, cmv, bd, W, s)
    y_rows.append(_retrieve_finish(pending, W))
    y_prompt = jnp.stack(y_rows)
    y_sample = _retrieve_finish(started_s, W).reshape(bd, s, D_MODEL)
    conv_state_s = utails.reshape(bd, SUBLANES, CONV_WIDTH)[:, SUBLANES - (CONV_K - 1):][None]

    return (y_prompt, y_sample,
            heads(jnp.stack(kf_rows), b, t), heads(jnp.stack(vf_rows), b, t), jnp.stack(logf_rows)[None],
            mk.reshape(1, b, mem_len, XA_HEADS, XA_HEAD_DIM), mv.reshape(1, b, mem_len, XA_HEADS, XA_HEAD_DIM),
            conv_state_p,
            heads(kfs, bd, s), heads(vfs, bd, s), logfs.reshape(1, bd, s, FOX_HEADS), conv_state_s)
```

```python
import functools

import jax
import jax.numpy as jnp
from jax import lax
from jax.experimental import pallas as pl
from jax.experimental.pallas import tpu as pltpu
from jax.experimental.pallas import tpu_sc as plsc

F32 = jnp.float32
BF16 = jnp.bfloat16
I32 = jnp.int32

D_MODEL = 1024
FOX_HEADS = 8
FOX_HEAD_DIM = 64
FOX_WIDTH = FOX_HEADS * FOX_HEAD_DIM
CONV_WIDTH = D_MODEL // 2
CONV_K = 3
PAGE_SIZE = 128
XA_HEADS = 4
XA_HEAD_DIM = D_MODEL // XA_HEADS
PEER_HEADS = 8
PEER_N_KEYS = 128
PEER_HALF = 64
PEER_TOPK = 16
PEER_SLOTS = PEER_HEADS * PEER_TOPK
PEER_TILE = 128
RMS_EPS = 1e-6

LANES = 128
SUBLANES = 8
VMEM_LIMIT = 56 * 1024 * 1024
NEG_INF = float("-inf")


def _cparams(sem):
    return pltpu.CompilerParams(dimension_semantics=sem, vmem_limit_bytes=VMEM_LIMIT)


def _full(shape):
    return pl.BlockSpec(shape, lambda *_: (0,) * len(shape))


def _rmsnorm(x, g):
    return x * lax.rsqrt(jnp.mean(x * x, axis=-1, keepdims=True) + RMS_EPS) * g


def _log_sigmoid(x):
    return jnp.minimum(x, 0.0) - jnp.log1p(jnp.exp(-jnp.abs(x)))


def _in_proj_body(x_ref, g_ref, wqkv_ref, wf_ref, wc_ref, wg_ref, bf_ref, wconv_ref, prev_ref,
                  q_ref, kf_ref, vf_ref, kb_ref, vb_ref, logf_ref, cbz_ref, sga_ref, sgb_ref, utail_ref,
                  carry_ref, *, seq_rows):
    tm = x_ref.shape[0]
    xn = _rmsnorm(x_ref[...], g_ref[...]).astype(BF16)

    qkv = jnp.dot(xn, wqkv_ref[...], preferred_element_type=F32)
    k = qkv[:, FOX_WIDTH:2 * FOX_WIDTH]
    v = qkv[:, 2 * FOX_WIDTH:]
    q_ref[...] = (qkv[:, :FOX_WIDTH] * (FOX_HEAD_DIM ** -0.5)).astype(BF16)
    kf_ref[...] = k
    vf_ref[...] = v
    kb_ref[...] = k.astype(BF16)
    vb_ref[...] = v.astype(BF16)

    fl = jnp.dot(xn, wf_ref[...], preferred_element_type=F32)
    logf_ref[...] = _log_sigmoid(fl[:, :FOX_HEADS] + bf_ref[...])

    c3 = jnp.dot(xn, wc_ref[...], preferred_element_type=F32)
    u = c3[:, CONV_WIDTH:2 * CONV_WIDTH] * c3[:, :CONV_WIDTH]
    cb = c3[:, 2 * CONV_WIDTH:]
    rows = lax.broadcasted_iota(I32, (tm, CONV_WIDTH), 0)
    if seq_rows is None:
        @pl.when(pl.program_id(1) == 0)
        def _():
            carry_ref[...] = prev_ref[...]
        hist = carry_ref[...]
        u1 = jnp.where(rows == 0, hist[7:8], pltpu.roll(u, 1, 0))
        u2 = jnp.where(rows == 0, hist[6:7], jnp.where(rows == 1, hist[7:8], pltpu.roll(u, 2, 0)))
        carry_ref[...] = u[tm - SUBLANES:]
        utail_ref[...] = u[tm - SUBLANES:]
    else:
        t = rows % seq_rows
        prev = prev_ref[...]
        u1 = jnp.where(t == 0, pltpu.roll(prev, tm - (seq_rows - 1), 0), pltpu.roll(u, 1, 0))
        u2 = jnp.where(t < 2, pltpu.roll(prev, tm - (seq_rows - 2), 0), pltpu.roll(u, 2, 0))
        utail_ref[...] = u
    wconv = wconv_ref[...]
    conv = u2 * wconv[0:1] + u1 * wconv[1:2] + u * wconv[2:3]
    cbz_ref[...] = (cb * conv).astype(BF16)

    gates = jnp.dot(xn, wg_ref[...], preferred_element_type=F32)
    sga_ref[...] = jax.nn.sigmoid(gates[:, :D_MODEL])
    sgb_ref[...] = jax.nn.sigmoid(gates[:, D_MODEL:])


def _prep_in_weights(w_in):
    o_f = 3 * FOX_WIDTH
    o_c = o_f + FOX_HEADS
    o_g = o_c + 3 * CONV_WIDTH
    wqkv = w_in[:, :o_f].astype(BF16)
    wf = jnp.pad(w_in[:, o_f:o_c], ((0, 0), (0, LANES - FOX_HEADS))).astype(BF16)
    wc = w_in[:, o_c:o_g].astype(BF16)
    wg = w_in[:, o_g:].astype(BF16)
    return wqkv, wf, wc, wg


def _in_proj(x2, g_mix, wqkv, wf, wc, wg, b_forget, w_conv, prev8, *, n_seq, seq_len, tm):
    n = x2.shape[0]
    if seq_len >= tm:
        assert seq_len % tm == 0
        nt = seq_len // tm
        grid = (n_seq, nt)
        tok = lambda w: pl.BlockSpec((tm, w), lambda b, t: (b * nt + t, 0))
        seq8 = pl.BlockSpec((SUBLANES, CONV_WIDTH), lambda b, t: (b, 0))
        seq_rows = None
    else:
        assert seq_len == SUBLANES and tm == n
        grid = (1, 1)
        tok = lambda w: pl.BlockSpec((tm, w), lambda b, t: (0, 0))
        seq8 = pl.BlockSpec((tm, CONV_WIDTH), lambda b, t: (0, 0))
        seq_rows = seq_len
    out_shape = (
        jax.ShapeDtypeStruct((n, FOX_WIDTH), BF16),
        jax.ShapeDtypeStruct((n, FOX_WIDTH), F32),
        jax.ShapeDtypeStruct((n, FOX_WIDTH), F32),
        jax.ShapeDtypeStruct((n, FOX_WIDTH), BF16),
        jax.ShapeDtypeStruct((n, FOX_WIDTH), BF16),
        jax.ShapeDtypeStruct((n, FOX_HEADS), F32),
        jax.ShapeDtypeStruct((n, CONV_WIDTH), BF16),
        jax.ShapeDtypeStruct((n, D_MODEL), F32),
        jax.ShapeDtypeStruct((n, D_MODEL), F32),
        jax.ShapeDtypeStruct((n_seq * SUBLANES, CONV_WIDTH), F32),
    )
    out_specs = (tok(FOX_WIDTH), tok(FOX_WIDTH), tok(FOX_WIDTH), tok(FOX_WIDTH), tok(FOX_WIDTH),
                 tok(FOX_HEADS), tok(CONV_WIDTH), tok(D_MODEL), tok(D_MODEL), seq8)
    in_specs = [tok(D_MODEL), _full((1, D_MODEL)), _full(wqkv.shape), _full(wf.shape), _full(wc.shape),
                _full(wg.shape), _full((1, FOX_HEADS)), _full((CONV_K, CONV_WIDTH)), seq8]
    return pl.pallas_call(
        functools.partial(_in_proj_body, seq_rows=seq_rows),
        grid=grid, in_specs=in_specs, out_specs=out_specs, out_shape=out_shape,
        scratch_shapes=[pltpu.VMEM((SUBLANES, CONV_WIDTH), F32)],
        compiler_params=_cparams(("arbitrary", "arbitrary")),
        name="in_proj",
    )(x2, g_mix.reshape(1, D_MODEL), wqkv, wf, wc, wg, b_forget.reshape(1, FOX_HEADS), w_conv, prev8)


def _lane_prefix_scan(x, lane):
    s = x
    sh = 1
    while sh < LANES:
        s = s + jnp.where(lane >= sh, pltpu.roll(s, sh, 1), 0.0)
        sh *= 2
    return s


def _cumsum_body(x_ref, o_ref):
    n_chunks = x_ref.shape[2] // LANES
    lane = lax.broadcasted_iota(I32, (FOX_HEADS, LANES), 1)

    def step(c, carry):
        off = pl.multiple_of(c * LANES, LANES)
        s = _lane_prefix_scan(x_ref[0, :, pl.ds(off, LANES)], lane) + carry
        o_ref[0, :, pl.ds(off, LANES)] = s
        return s[:, LANES - 1:]

    lax.fori_loop(0, n_chunks, step, jnp.zeros((FOX_HEADS, 1), F32))


def _cumsum_time(logf_t):
    b, h, t = logf_t.shape
    spec = pl.BlockSpec((1, h, t), lambda i: (i, 0, 0))
    return pl.pallas_call(
        _cumsum_body, grid=(b,), in_specs=[spec], out_specs=spec,
        out_shape=jax.ShapeDtypeStruct(logf_t.shape, F32),
        compiler_params=_cparams(("arbitrary",)), name="forget_cumsum",
    )(logf_t)


def _fox_prompt_body(q_ref, k_ref, v_ref, fcol_ref, frow_ref, o_ref, *, tq):
    hp = pl.program_id(1)
    qi = pl.program_id(2)
    q2 = q_ref[0]
    lane = lax.broadcasted_iota(I32, (tq, LANES), 1)
    upper = lane >= FOX_HEAD_DIM
    col8 = lax.broadcasted_iota(I32, (tq, FOX_HEADS), 1)
    fcol8 = fcol_ref[0]
    rowi = lax.broadcasted_iota(I32, (tq, tq), 0)
    coli = lax.broadcasted_iota(I32, (tq, tq), 1)
    zero = jnp.zeros_like(q2)

    outs = []
    for hh in range(2):
        h = 2 * hp + hh
        qh = jnp.where(upper, q2, zero) if hh else jnp.where(upper, zero, q2)
        fq = jnp.sum(jnp.where(col8 == h, fcol8, 0.0), axis=1, keepdims=True)

        def block(kb, carry, masked):
            m, l, acc = carry
            off = pl.multiple_of(kb * tq, tq)
            kk = k_ref[0, pl.ds(off, tq), :]
            vv = v_ref[0, pl.ds(off, tq), :]
            fk = frow_ref[0, pl.ds(h, 1), pl.ds(off, tq)]
            s = lax.dot_general(qh, kk, (((1,), (1,)), ((), ())), preferred_element_type=F32)
            s = s + fq - fk
            if masked:
                s = jnp.where(coli <= rowi, s, NEG_INF)
            m_new = jnp.maximum(m, jnp.max(s, axis=1, keepdims=True))
            p = jnp.exp(s - m_new)
            alpha = jnp.exp(m - m_new)
            l = alpha * l + jnp.sum(p, axis=1, keepdims=True)
            acc = alpha * acc + jnp.dot(p.astype(BF16), vv, preferred_element_type=F32)
            return m_new, l, acc

        init = (jnp.full((tq, 1), NEG_INF, F32), jnp.zeros((tq, 1), F32), jnp.zeros((tq, LANES), F32))
        carry = lax.fori_loop(0, qi, functools.partial(block, masked=False), init)
        m, l, acc = block(qi, carry, True)
        outs.append(acc / l)
    o_ref[0] = jnp.where(upper, outs[1], outs[0]).astype(o_ref.dtype)


def _fox_prompt(q, k, v, fcol, frow, *, tq):
    b, t, _ = q.shape
    n_pairs = FOX_WIDTH // LANES
    return pl.pallas_call(
        functools.partial(_fox_prompt_body, tq=tq),
        grid=(b, n_pairs, t // tq),
        in_specs=[
            pl.BlockSpec((1, tq, LANES), lambda i, p, j: (i, j, p)),
            pl.BlockSpec((1, t, LANES), lambda i, p, j: (i, 0, p)),
            pl.BlockSpec((1, t, LANES), lambda i, p, j: (i, 0, p)),
            pl.BlockSpec((1, tq, FOX_HEADS), lambda i, p, j: (i, j, 0)),
            pl.BlockSpec((1, FOX_HEADS, t), lambda i, p, j: (i, 0, 0)),
        ],
        out_specs=pl.BlockSpec((1, tq, LANES), lambda i, p, j: (i, j, p)),
        out_shape=jax.ShapeDtypeStruct((b, t, FOX_WIDTH), BF16),
        compiler_params=_cparams(("arbitrary", "arbitrary", "arbitrary")),
        name="fox_prompt",
    )(q, k, v, fcol, frow)


SAMPLE_ROWS = 64


def _lane_suffix_excl(x, lane):
    s = jnp.where(lane < LANES - 1, pltpu.roll(x, LANES - 1, 1), 0.0)
    sh = 1
    while sh < LANES:
        s = s + jnp.where(lane < LANES - sh, pltpu.roll(s, LANES - sh, 1), 0.0)
        sh *= 2
    return s


def _fox_sample_body(pt_ref, q_ref, kn_ref, vn_ref, lfcol_ref, lfrow_ref, *rest, pages_per_step):
    pg = pages_per_step
    k_refs = rest[:pg]
    v_refs = rest[pg:2 * pg]
    lf_refs = rest[2 * pg:3 * pg]
    o_ref = rest[3 * pg]
    m_ref, l_ref, acc_ref, tot_ref, qbd_ref, rt_ref = rest[3 * pg + 1:]
    g = pl.program_id(1)
    lane8 = lax.broadcasted_iota(I32, (FOX_HEADS, LANES), 1)
    row = lax.broadcasted_iota(I32, (SAMPLE_ROWS, FOX_WIDTH), 0)
    lane = lax.broadcasted_iota(I32, (SAMPLE_ROWS, FOX_WIDTH), 1)
    head_mask = (lane // FOX_HEAD_DIM) == (row % FOX_HEADS)
    nt = (((1,), (1,)), ((), ()))

    def tile8(x):
        return jnp.concatenate([x] * (SAMPLE_ROWS // FOX_HEADS), axis=0)

    @pl.when(g == 0)
    def _():
        q = q_ref[0]
        qrows = jnp.concatenate(
            [jnp.broadcast_to(q[t:t + 1, :], (FOX_HEADS, FOX_WIDTH)) for t in range(q.shape[0])], axis=0)
        qbd = jnp.where(head_mask, qrows, jnp.zeros_like(qrows))
        qbd_ref[...] = qbd
        r1 = lax.broadcasted_iota(I32, (SAMPLE_ROWS, 1), 0)
        x = lfcol_ref[0]
        s = jnp.where(r1 < SAMPLE_ROWS - 8, pltpu.roll(x, SAMPLE_ROWS - 8, 0), 0.0)
        for sh in (8, 16, 32):
            s = s + jnp.where(r1 < SAMPLE_ROWS - sh, pltpu.roll(s, SAMPLE_ROWS - sh, 0), 0.0)
        rt = -s
        rt_ref[...] = rt
        lfn = lfrow_ref[0]
        excl = _lane_suffix_excl(lfn, lane8)
        tot_ref[...] = excl[:, 0:1] + lfn[:, 0:1]
        sc = jnp.dot(qbd, kn_ref[0], preferred_element_type=F32) + rt + tile8(excl)
        rr = lax.broadcasted_iota(I32, (SAMPLE_ROWS, LANES), 0)
        cc = lax.broadcasted_iota(I32, (SAMPLE_ROWS, LANES), 1)
        sc = jnp.where(cc <= rr // FOX_HEADS, sc, NEG_INF)
        m_new = jnp.max(sc, axis=1, keepdims=True)
        p = jnp.exp(sc - m_new)
        m_ref[...] = m_new
        l_ref[...] = jnp.sum(p, axis=1, keepdims=True)
        acc_ref[...] = lax.dot_general(p.astype(BF16), vn_ref[0], nt, preferred_element_type=F32)

    qbd = qbd_ref[...]
    tot = tot_ref[...]
    biases, ks, vs = [], [], []
    for i in reversed(range(pg)):
        lfp = lf_refs[i][0]
        r = _lane_suffix_excl(lfp, lane8) + tot
        tot = r[:, 0:1] + lfp[:, 0:1]
        biases.append(tile8(r))
        ks.append(k_refs[i][0].reshape(FOX_WIDTH, PAGE_SIZE).astype(BF16))
        vs.append(v_refs[i][0].reshape(FOX_WIDTH, PAGE_SIZE).astype(BF16))
    tot_ref[...] = tot
    sc = jnp.dot(qbd, jnp.concatenate(ks, axis=1), preferred_element_type=F32)
    sc = sc + rt_ref[...] + jnp.concatenate(biases, axis=1)
    m_old = m_ref[...]
    m_new = jnp.maximum(m_old, jnp.max(sc, axis=1, keepdims=True))
    p = jnp.exp(sc - m_new)
    alpha = jnp.exp(m_old - m_new)
    l_ref[...] = alpha * l_ref[...] + jnp.sum(p, axis=1, keepdims=True)
    acc_ref[...] = alpha * acc_ref[...] + lax.dot_general(
        p.astype(BF16), jnp.concatenate(vs, axis=1), nt, preferred_element_type=F32)
    m_ref[...] = m_new

    @pl.when(g == pl.num_programs(1) - 1)
    def _():
        full = jnp.where(head_mask, acc_ref[...] / l_ref[...], 0.0)
        o_ref[0] = jnp.concatenate(
            [jnp.sum(full[FOX_HEADS * t:FOX_HEADS * (t + 1)], axis=0, keepdims=True)
             for t in range(SAMPLE_ROWS // FOX_HEADS)], axis=0).astype(o_ref.dtype)


def _fox_sample(page_table, q, kn_t, vn_t, lfcol, lfrow_pad, cache_kt, cache_vt, cache_lf_t, *, pages_per_step):
    bd, n_pages = page_table.shape
    pg = pages_per_step
    assert n_pages % pg == 0
    ng = n_pages // pg
    s_new = q.shape[1]

    def page_spec(shape, i):
        zeros = (0,) * (len(shape) - 1)
        return pl.BlockSpec(shape, lambda b, g, pt: (pt[b, (ng - 1 - g) * pg + i],) + zeros)

    req = lambda shape: pl.BlockSpec(shape, lambda b, g, pt: (b, 0, 0))
    in_specs = [req((1, s_new, FOX_WIDTH)), req((1, FOX_WIDTH, PAGE_SIZE)), req((1, FOX_WIDTH, PAGE_SIZE)),
                req((1, SAMPLE_ROWS, 1)), req((1, FOX_HEADS, LANES))]
    in_specs += [page_spec((1, FOX_HEADS, FOX_HEAD_DIM, PAGE_SIZE), i) for i in range(pg)]
    in_specs += [page_spec((1, FOX_HEADS, FOX_HEAD_DIM, PAGE_SIZE), i) for i in range(pg)]
    in_specs += [page_spec((1, FOX_HEADS, PAGE_SIZE), i) for i in range(pg)]
    grid_spec = pltpu.PrefetchScalarGridSpec(
        num_scalar_prefetch=1, grid=(bd, ng), in_specs=in_specs,
        out_specs=req((1, s_new, FOX_WIDTH)),
        scratch_shapes=[pltpu.VMEM((SAMPLE_ROWS, 1), F32), pltpu.VMEM((SAMPLE_ROWS, 1), F32),
                        pltpu.VMEM((SAMPLE_ROWS, FOX_WIDTH), F32), pltpu.VMEM((FOX_HEADS, 1), F32),
                        pltpu.VMEM((SAMPLE_ROWS, FOX_WIDTH), BF16), pltpu.VMEM((SAMPLE_ROWS, 1), F32)])
    return pl.pallas_call(
        functools.partial(_fox_sample_body, pages_per_step=pg),
        grid_spec=grid_spec, out_shape=jax.ShapeDtypeStruct((bd, s_new, FOX_WIDTH), BF16),
        compiler_params=_cparams(("arbitrary", "arbitrary")), name="fox_sample",
    )(page_table, q, kn_t, vn_t, lfcol, lfrow_pad, *([cache_kt] * pg), *([cache_vt] * pg), *([cache_lf_t] * pg))


def _merge_body(x_ref, attn_ref, cbz_ref, sga_ref, sgb_ref, wfo_ref, wco_ref, wo_ref, gx_ref, wxq_ref,
                h1_ref, qx_ref):
    ya = jnp.dot(attn_ref[...], wfo_ref[...], preferred_element_type=F32)
    yb = jnp.dot(cbz_ref[...], wco_ref[...], preferred_element_type=F32)
    mix = (sga_ref[...] * ya + sgb_ref[...] * yb).astype(BF16)
    h1 = x_ref[...] + jnp.dot(mix, wo_ref[...], preferred_element_type=F32)
    h1_ref[...] = h1
    xn = _rmsnorm(h1, gx_ref[...]).astype(BF16)
    qx_ref[...] = (jnp.dot(xn, wxq_ref[...], preferred_element_type=F32) * (XA_HEAD_DIM ** -0.5)).astype(BF16)


def _merge(x2, attn, cbz, sga, sgb, wfo, wco, wo, g_xattn, wxq, *, tm):
    n = x2.shape[0]
    tok = lambda w: pl.BlockSpec((tm, w), lambda i: (i, 0))
    return pl.pallas_call(
        _merge_body, grid=(n // tm,),
        in_specs=[tok(D_MODEL), tok(FOX_WIDTH), tok(CONV_WIDTH), tok(D_MODEL), tok(D_MODEL),
                  _full(wfo.shape), _full(wco.shape), _full(wo.shape), _full((1, D_MODEL)), _full(wxq.shape)],
        out_specs=(tok(D_MODEL), tok(D_MODEL)),
        out_shape=(jax.ShapeDtypeStruct((n, D_MODEL), F32), jax.ShapeDtypeStruct((n, D_MODEL), BF16)),
        compiler_params=_cparams(("arbitrary",)), name="merge",
    )(x2, attn, cbz, sga, sgb, wfo, wco, wo, g_xattn.reshape(1, D_MODEL), wxq)


def _memory_kv_body(m_ref, g_ref, wk_ref, wv_ref, mk_ref, mv_ref):
    mn = _rmsnorm(m_ref[...], g_ref[...]).astype(BF16)
    mk_ref[...] = jnp.dot(mn, wk_ref[...], preferred_element_type=F32)
    mv_ref[...] = jnp.dot(mn, wv_ref[...], preferred_element_type=F32)


def _memory_kv(mem2, g_mem, wxk, wxv, *, tm):
    n = mem2.shape[0]
    tok = pl.BlockSpec((tm, D_MODEL), lambda i: (i, 0))
    return pl.pallas_call(
        _memory_kv_body, grid=(n // tm,),
        in_specs=[tok, _full((1, D_MODEL)), _full(wxk.shape), _full(wxv.shape)],
        out_specs=(tok, tok),
        out_shape=(jax.ShapeDtypeStruct((n, D_MODEL), F32),) * 2,
        compiler_params=_cparams(("arbitrary",)), name="memory_kv",
    )(mem2, g_mem.reshape(1, D_MODEL), wxk, wxv)


def _cross_attn_body(q_ref, mk_ref, mv_ref, o_ref):
    q = q_ref[0]
    outs = []
    for h in range(XA_HEADS):
        sl = slice(h * XA_HEAD_DIM, (h + 1) * XA_HEAD_DIM)
        kh = mk_ref[0, :, sl].astype(BF16)
        vh = mv_ref[0, :, sl].astype(BF16)
        s = lax.dot_general(q[:, sl], kh, (((1,), (1,)), ((), ())), preferred_element_type=F32)
        p = jnp.exp(s - jnp.max(s, axis=1, keepdims=True))
        p = p / jnp.sum(p, axis=1, keepdims=True)
        outs.append(jnp.dot(p.astype(BF16), vh, preferred_element_type=F32))
    o_ref[0] = jnp.concatenate(outs, axis=1).astype(o_ref.dtype)


def _cross_attn(qx, mk, mv, *, tm):
    b, t, _ = qx.shape
    m = mk.shape[1]
    tok = pl.BlockSpec((1, tm, D_MODEL), lambda i, j: (i, j, 0))
    mem = pl.BlockSpec((1, m, D_MODEL), lambda i, j: (i, 0, 0))
    return pl.pallas_call(
        _cross_attn_body, grid=(b, t // tm), in_specs=[tok, mem, mem], out_specs=tok,
        out_shape=jax.ShapeDtypeStruct((b, t, D_MODEL), BF16),
        compiler_params=_cparams(("arbitrary", "arbitrary")), name="cross_attn",
    )(qx, mk, mv)


PEER_PAIRS = tuple((i, j) for i in range(PEER_TOPK) for j in range(PEER_TOPK) if (i + 1) * (j + 1) <= PEER_TOPK)
PEER_CAND_ROWS = -(-len(PEER_PAIRS) // SUBLANES) * SUBLANES


def _top_rows(s, k, payload=None):
    n = s.shape[0]
    iota = lax.broadcasted_iota(I32, s.shape, 0)
    vals, ids = [], []
    for _ in range(k):
        m = jnp.max(s, axis=0, keepdims=True)
        r = jnp.min(jnp.where(s == m, iota, n), axis=0, keepdims=True)
        hit = iota == r
        vals.append(m)
        ids.append(r if payload is None else jnp.max(jnp.where(hit, payload, -1), axis=0, keepdims=True))
        s = jnp.where(hit, NEG_INF, s)
    return vals, ids


def _peer_front_body(h1_ref, o_ref, wxo_ref, gp_ref, wpq_ref, ka_ref, kb_ref,
                     h2_ref, xn_ref, eidx_ref, gate_ref, qp_ref, e_ref, g_ref):
    tm = h1_ref.shape[0]
    h2 = h1_ref[...] + jnp.dot(o_ref[...], wxo_ref[...], preferred_element_type=F32)
    h2_ref[...] = h2
    xn = _rmsnorm(h2, gp_ref[...]).astype(BF16)
    xn_ref[...] = xn
    qp_ref[...] = jnp.dot(xn, wpq_ref[...], preferred_element_type=F32).astype(BF16)

    def head(h, _):
        qh = qp_ref[:, pl.ds(pl.multiple_of(h * LANES, LANES), LANES)]
        nt = (((1,), (1,)), ((), ()))
        sa = lax.dot_general(ka_ref[h], qh, nt, preferred_element_type=F32)
        sb = lax.dot_general(kb_ref[h], qh, nt, preferred_element_type=F32)
        va, ia = _top_rows(sa, PEER_TOPK)
        vb, ib = _top_rows(sb, PEER_TOPK)
        pad = PEER_CAND_ROWS - len(PEER_PAIRS)
        cand = jnp.concatenate([va[i] + vb[j] for i, j in PEER_PAIRS]
                               + [jnp.full((pad, tm), NEG_INF, F32)], axis=0)
        ceid = jnp.concatenate([ia[i] * PEER_N_KEYS + ib[j] for i, j in PEER_PAIRS]
                               + [jnp.zeros((pad, tm), I32)], axis=0)
        ts, te = _top_rows(cand, PEER_TOPK, payload=ceid)
        ex = [jnp.exp(t - ts[0]) for t in ts]
        den = ex[0]
        for e in ex[1:]:
            den = den + e
        row0 = pl.multiple_of(h * PEER_TOPK, PEER_TOPK)
        e_ref[pl.ds(row0, PEER_TOPK), :] = jnp.concatenate(te, axis=0)
        g_ref[pl.ds(row0, PEER_TOPK), :] = jnp.concatenate([e / den for e in ex], axis=0)
        return 0

    lax.fori_loop(0, PEER_HEADS, head, 0)
    for c in range(eidx_ref.shape[0]):
        eidx_ref[c] = e_ref[:, c * PEER_TILE:(c + 1) * PEER_TILE]
    gate_ref[...] = g_ref[...].T


def _prep_peer_keys(keys_a, keys_b):
    z = jnp.zeros_like(keys_a)
    return (jnp.concatenate([keys_a, z], axis=-1).astype(BF16), jnp.concatenate([z, keys_b], axis=-1).astype(BF16))


def _peer_front(h1, o, wxo, g_peer, wpq, ka_pad, kb_pad, *, tm):
    n = h1.shape[0]
    assert tm % PEER_TILE == 0
    sub = tm // PEER_TILE
    tok = lambda w: pl.BlockSpec((tm, w), lambda i: (i, 0))
    return pl.pallas_call(
        _peer_front_body, grid=(n // tm,),
        in_specs=[tok(D_MODEL), tok(D_MODEL), _full(wxo.shape), _full((1, D_MODEL)), _full(wpq.shape),
                  _full(ka_pad.shape), _full(kb_pad.shape)],
        out_specs=(tok(D_MODEL), tok(D_MODEL), pl.BlockSpec((sub, PEER_SLOTS, PEER_TILE), lambda i: (i, 0, 0)),
                   tok(PEER_SLOTS)),
        out_shape=(jax.ShapeDtypeStruct((n, D_MODEL), F32), jax.ShapeDtypeStruct((n, D_MODEL), BF16),
                   jax.ShapeDtypeStruct((n // PEER_TILE, PEER_SLOTS, PEER_TILE), I32),
                   jax.ShapeDtypeStruct((n, PEER_SLOTS), F32)),
        scratch_shapes=[pltpu.VMEM((tm, PEER_HEADS * LANES), BF16), pltpu.VMEM((PEER_SLOTS, tm), I32),
                        pltpu.VMEM((PEER_SLOTS, tm), F32)],
        compiler_params=_cparams(("arbitrary",)), name="peer_front",
    )(h1, o, wxo, g_peer.reshape(1, D_MODEL), wpq, ka_pad, kb_pad)


PACKED_WIDTH = D_MODEL // 2


def _pack_table(t):
    b = lax.bitcast_convert_type(t.astype(BF16), jnp.uint16).astype(jnp.uint32)
    return lax.bitcast_convert_type(b[:, :PACKED_WIDTH] | (b[:, PACKED_WIDTH:] << 16), I32)


def _unpack(w):
    lo = lax.bitcast_convert_type(w << 16, F32)
    hi = lax.bitcast_convert_type(w & jnp.int32(-65536), F32)
    return lo, hi


SC_CORES = 2
SC_SUBCORES = 16
SC_WINDOW = 64


def _sc_gather2(table_u, table_v, idx):
    r = idx.shape[0]
    workers = SC_CORES * SC_SUBCORES
    assert r % (workers * SC_WINDOW) == 0
    per_worker = r // workers
    n_win = per_worker // SC_WINDOW
    width = table_u.shape[1]
    mesh = plsc.VectorSubcoreMesh(core_axis_name="c", subcore_axis_name="s")
    out = jax.ShapeDtypeStruct((r, width), table_u.dtype)

    @functools.partial(
        pl.kernel, mesh=mesh, out_type=(out, out),
        scratch_types=[pltpu.VMEM((SC_WINDOW,), I32), pltpu.VMEM((SC_WINDOW, width), table_u.dtype),
                       pltpu.VMEM((SC_WINDOW, width), table_u.dtype),
                       pltpu.SemaphoreType.DMA, pltpu.SemaphoreType.DMA])
    def gather(u_hbm, v_hbm, idx_hbm, ou_hbm, ov_hbm, idx_v, ru_v, rv_v, sem_u, sem_v):
        wid = lax.axis_index("s") * SC_CORES + lax.axis_index("c")
        base = wid * per_worker

        @pl.loop(0, n_win)
        def _(i):
            off = base + i * SC_WINDOW
            pltpu.sync_copy(idx_hbm.at[pl.ds(off, SC_WINDOW)], idx_v)
            cu = pltpu.async_copy(u_hbm.at[idx_v], ru_v, sem_u)
            cv = pltpu.async_copy(v_hbm.at[idx_v], rv_v, sem_v)
            cu.wait()
            pltpu.sync_copy(ru_v, ou_hbm.at[pl.ds(off, SC_WINDOW)])
            cv.wait()
            pltpu.sync_copy(rv_v, ov_hbm.at[pl.ds(off, SC_WINDOW)])

    return gather(table_u, table_v, idx)


def _peer_back_body(ug_ref, vg_ref, xn_ref, gate_ref, h2_ref, gf_ref, y_ref, x32_ref, out_ref, *, slots_per_step):
    sg = pl.program_id(1)
    tp = xn_ref.shape[0]
    n_groups = tp // SUBLANES
    lane = lax.broadcasted_iota(I32, (SUBLANES, PEER_SLOTS), 1)
    slot0 = sg * slots_per_step

    @pl.when(sg == 0)
    def _():
        x32_ref[...] = xn_ref[...].astype(F32)
        out_ref[...] = jnp.zeros_like(out_ref)

    def rows_of(r):
        return pl.ds(pl.multiple_of(r * SUBLANES, SUBLANES), SUBLANES)

    def u_pass(r):
        rows = rows_of(r)
        xlo = x32_ref[rows, :PACKED_WIDTH]
        xhi = x32_ref[rows, PACKED_WIDTH:]
        act = jnp.zeros((SUBLANES, PEER_SLOTS), F32)
        for j in range(slots_per_step):
            ulo, uhi = _unpack(ug_ref[0, j, rows, :])
            a = jnp.sum(ulo * xlo + uhi * xhi, axis=1, keepdims=True)
            act = jnp.where(lane == slot0 + j, a, act)
        return gate_ref[rows, :] * jax.nn.gelu(act)

    def v_pass(r, wts):
        rows = rows_of(r)
        olo = out_ref[rows, :PACKED_WIDTH]
        ohi = out_ref[rows, PACKED_WIDTH:]
        for j in range(slots_per_step):
            w = jnp.sum(jnp.where(lane == slot0 + j, wts, 0.0), axis=1, keepdims=True)
            vlo, vhi = _unpack(vg_ref[0, j, rows, :])
            olo = olo + w * vlo
            ohi = ohi + w * vhi
        out_ref[rows, :PACKED_WIDTH] = olo
        out_ref[rows, PACKED_WIDTH:] = ohi

    def group(r, wts):
        nxt = u_pass(r)
        v_pass(r - 1, wts)
        return nxt

    last = lax.fori_loop(1, n_groups, group, u_pass(0))
    v_pass(n_groups - 1, last)

    @pl.when(sg == pl.num_programs(1) - 1)
    def _():
        y_ref[...] = _rmsnorm(h2_ref[...] + out_ref[...], gf_ref[...])


def _peer_back(ug, vg, xn, gate, h2, g_final, *, tp, slots_per_step):
    n = xn.shape[0]
    n_sg = PEER_SLOTS // slots_per_step
    rows = pl.BlockSpec((1, slots_per_step, tp, PACKED_WIDTH), lambda i, s: (i, s, 0, 0))
    tok = lambda w: pl.BlockSpec((tp, w), lambda i, s: (i, 0))
    return pl.pallas_call(
        functools.partial(_peer_back_body, slots_per_step=slots_per_step),
        grid=(n // tp, n_sg),
        in_specs=[rows, rows, tok(D_MODEL), tok(PEER_SLOTS), tok(D_MODEL), _full((1, D_MODEL))],
        out_specs=tok(D_MODEL),
        out_shape=jax.ShapeDtypeStruct((n, D_MODEL), F32),
        scratch_shapes=[pltpu.VMEM((tp, D_MODEL), F32), pltpu.VMEM((tp, D_MODEL), F32)],
        compiler_params=_cparams(("arbitrary", "arbitrary")), name="peer_back",
    )(ug, vg, xn, gate, h2, g_final.reshape(1, D_MODEL))


PROJ_TILE = 256
ATTN_TILE = 512
PEER_FRONT_TILE = 256
PEER_SLOT_STEP = 32
SAMPLE_PAGES_PER_STEP = 16


def _retrieve_start(x2, attn, cbz, sga, sgb, mk, mv, n_batch, W, xa_tile):
    n = x2.shape[0]
    h1, qx = _merge(x2, attn, cbz, sga, sgb, W["wfo"], W["wco"], W["wo"], W["g_xattn"], W["wxq"], tm=PROJ_TILE)
    o = _cross_attn(qx.reshape(n_batch, n // n_batch, D_MODEL), mk, mv, tm=xa_tile).reshape(n, D_MODEL)
    h2, xn, eidx, gate = _peer_front(h1, o, W["wxo"], W["g_peer"], W["wpq"], W["ka"], W["kb"], tm=PEER_FRONT_TILE)
    ug, vg = _sc_gather2(W["pu"], W["pv"], eidx.reshape(-1))
    return ug, vg, xn, gate, h2


def _retrieve_finish(pending, W):
    ug, vg, xn, gate, h2 = pending
    shape4 = (xn.shape[0] // PEER_TILE, PEER_SLOTS, PEER_TILE, PACKED_WIDTH)
    return _peer_back(ug.reshape(shape4), vg.reshape(shape4), xn, gate, h2, W["g_final"],
                      tp=PEER_TILE, slots_per_step=PEER_SLOT_STEP)


def kernel(x_prompt, x_sample, cache_fox_k, cache_fox_v, cache_fox_logf, cache_mem_k, cache_mem_v, state_conv,
           page_table, mem_prompt, g_mix, w_in, b_forget, w_conv, w_fox_out, w_conv_out, w_o, g_xattn, g_mem,
           w_xq, w_xk, w_xv, w_xo, g_peer, w_peer_q, peer_keys_a, peer_keys_b, peer_u, peer_v, g_final):
    depth = g_mix.shape[0]
    assert depth == 1
    l = 0
    b, t, _ = x_prompt.shape
    bd, s, _ = x_sample.shape
    n_pool = cache_fox_k.shape[1]
    mem_len = mem_prompt.shape[1]

    in_w = _prep_in_weights(w_in[l])
    ka, kb = _prep_peer_keys(peer_keys_a[l], peer_keys_b[l])
    W = dict(wfo=w_fox_out[l].astype(BF16), wco=w_conv_out[l].astype(BF16), wo=w_o[l].astype(BF16),
             g_xattn=g_xattn[l], wxq=w_xq[l].astype(BF16), wxo=w_xo[l].astype(BF16), g_peer=g_peer[l],
             wpq=w_peer_q[l].astype(BF16), ka=ka, kb=kb, pu=_pack_table(peer_u[l]), pv=_pack_table(peer_v[l]),
             g_final=g_final)

    mk, mv = _memory_kv(mem_prompt.reshape(b * mem_len, D_MODEL), g_mem[l], w_xk[l].astype(BF16),
                        w_xv[l].astype(BF16), tm=PROJ_TILE)
    mk3 = mk.reshape(b, mem_len, D_MODEL)
    mv3 = mv.reshape(b, mem_len, D_MODEL)
    prev0 = jnp.zeros((SUBLANES, CONV_WIDTH), F32)
    kf_rows, vf_rows, logf_rows, utail_rows, y_rows = [], [], [], [], []
    pending = None
    for r in range(b):
        q, kf, vf, kb16, vb16, logf, cbz, sga, sgb, utail = _in_proj(
            x_prompt[r], g_mix[l], *in_w, b_forget[l], w_conv[l], prev0, n_seq=1, seq_len=t, tm=PROJ_TILE)
        frow = _cumsum_time(logf.T[None])
        attn = _fox_prompt(q[None], kb16[None], vb16[None], frow.transpose(0, 2, 1), frow, tq=ATTN_TILE)[0]
        started = _retrieve_start(x_prompt[r], attn, cbz, sga, sgb, mk3[r:r + 1], mv3[r:r + 1], 1, W, ATTN_TILE)
        if pending is not None:
            y_rows.append(_retrieve_finish(pending, W))
        pending = started
        kf_rows.append(kf)
        vf_rows.append(vf)
        logf_rows.append(logf)
        utail_rows.append(utail)
    heads = lambda a, nb, nt: a.reshape(1, nb, nt, FOX_HEADS, FOX_HEAD_DIM)
    conv_state_p = jnp.stack(utail_rows)[:, SUBLANES - (CONV_K - 1):][None]

    xs = x_sample.reshape(bd * s, D_MODEL)
    prev_s = jnp.concatenate([jnp.zeros((bd, SUBLANES - (CONV_K - 1), CONV_WIDTH), F32), state_conv[l]],
                             axis=1).reshape(bd * SUBLANES, CONV_WIDTH)
    qs, kfs, vfs, kbs, vbs, logfs, cbzs, sgas, sgbs, utails = _in_proj(
        xs, g_mix[l], *in_w, b_forget[l], w_conv[l], prev_s, n_seq=bd, seq_len=s, tm=bd * s)
    assert s * FOX_HEADS == SAMPLE_ROWS
    pad_page = lambda a: jnp.pad(a.reshape(bd, s, FOX_WIDTH).transpose(0, 2, 1),
                                 ((0, 0), (0, 0), (0, PAGE_SIZE - s)))
    lf3 = logfs.reshape(bd, s, FOX_HEADS)
    lfcol = lf3.reshape(bd, SAMPLE_ROWS, 1)
    lfrow = jnp.pad(lf3.transpose(0, 2, 1), ((0, 0), (0, 0), (0, LANES - s)))
    page_view = lambda c: c.transpose(0, 2, 3, 1)
    attn_s = _fox_sample(
        page_table, qs.reshape(bd, s, FOX_WIDTH), pad_page(kbs), pad_page(vbs), lfcol, lfrow,
        page_view(cache_fox_k[l]), page_view(cache_fox_v[l]), cache_fox_logf[l].transpose(0, 2, 1),
        pages_per_step=SAMPLE_PAGES_PER_STEP).reshape(bd * s, FOX_WIDTH)
    cmk = cache_mem_k[l].reshape(bd, mem_len, D_MODEL)
    cmv = cache_mem_v[l].reshape(bd, mem_len, D_MODEL)
    started_s = _retrieve_start(xs, attn_s, cbzs, sgas, sgbs, cmk, cmv, bd, W, s)
    y_rows.append(_retrieve_finish(pending, W))
    y_prompt = jnp.stack(y_rows)
    y_sample = _retrieve_finish(started_s, W).reshape(bd, s, D_MODEL)
    conv_state_s = utails.reshape(bd, SUBLANES, CONV_WIDTH)[:, SUBLANES - (CONV_K - 1):][None]

    return (y_prompt, y_sample,
            heads(jnp.stack(kf_rows), b, t), heads(jnp.stack(vf_rows), b, t), jnp.stack(logf_rows)[None],
            mk.reshape(1, b, mem_len, XA_HEADS, XA_HEAD_DIM), mv.reshape(1, b, mem_len, XA_HEADS, XA_HEAD_DIM),
            conv_state_p,
            heads(kfs, bd, s), heads(vfs, bd, s), logfs.reshape(1, bd, s, FOX_HEADS), conv_state_s)
```

```python
import functools

import jax
import jax.numpy as jnp
from jax import lax
from jax.experimental import pallas as pl
from jax.experimental.pallas import tpu as pltpu
from jax.experimental.pallas import tpu_sc as plsc

F32 = jnp.float32
BF16 = jnp.bfloat16
I32 = jnp.int32

D_MODEL = 1024
FOX_HEADS = 8
FOX_HEAD_DIM = 64
FOX_WIDTH = FOX_HEADS * FOX_HEAD_DIM
CONV_WIDTH = D_MODEL // 2
CONV_K = 3
PAGE_SIZE = 128
XA_HEADS = 4
XA_HEAD_DIM = D_MODEL // XA_HEADS
PEER_HEADS = 8
PEER_N_KEYS = 128
PEER_HALF = 64
PEER_TOPK = 16
PEER_SLOTS = PEER_HEADS * PEER_TOPK
PEER_TILE = 128
RMS_EPS = 1e-6

LANES = 128
SUBLANES = 8
VMEM_LIMIT = 56 * 1024 * 1024
NEG_INF = float("-inf")


def _cparams(sem):
    return pltpu.CompilerParams(dimension_semantics=sem, vmem_limit_bytes=VMEM_LIMIT)


def _full(shape):
    return pl.BlockSpec(shape, lambda *_: (0,) * len(shape))


def _rmsnorm(x, g):
    return x * lax.rsqrt(jnp.mean(x * x, axis=-1, keepdims=True) + RMS_EPS) * g


def _log_sigmoid(x):
    return jnp.minimum(x, 0.0) - jnp.log1p(jnp.exp(-jnp.abs(x)))


def _in_proj_body(x_ref, g_ref, wqkv_ref, wf_ref, wc_ref, wg_ref, bf_ref, wconv_ref, prev_ref,
                  q_ref, kf_ref, vf_ref, kb_ref, vb_ref, logf_ref, cbz_ref, sga_ref, sgb_ref, utail_ref,
                  carry_ref, *, seq_rows):
    tm = x_ref.shape[0]
    xn = _rmsnorm(x_ref[...], g_ref[...]).astype(BF16)

    qkv = jnp.dot(xn, wqkv_ref[...], preferred_element_type=F32)
    k = qkv[:, FOX_WIDTH:2 * FOX_WIDTH]
    v = qkv[:, 2 * FOX_WIDTH:]
    q_ref[...] = (qkv[:, :FOX_WIDTH] * (FOX_HEAD_DIM ** -0.5)).astype(BF16)
    kf_ref[...] = k
    vf_ref[...] = v
    kb_ref[...] = k.astype(BF16)
    vb_ref[...] = v.astype(BF16)

    fl = jnp.dot(xn, wf_ref[...], preferred_element_type=F32)
    logf_ref[...] = _log_sigmoid(fl[:, :FOX_HEADS] + bf_ref[...])

    c3 = jnp.dot(xn, wc_ref[...], preferred_element_type=F32)
    u = c3[:, CONV_WIDTH:2 * CONV_WIDTH] * c3[:, :CONV_WIDTH]
    cb = c3[:, 2 * CONV_WIDTH:]
    rows = lax.broadcasted_iota(I32, (tm, CONV_WIDTH), 0)
    if seq_rows is None:
        @pl.when(pl.program_id(1) == 0)
        def _():
            carry_ref[...] = prev_ref[...]
        hist = carry_ref[...]
        u1 = jnp.where(rows == 0, hist[7:8], pltpu.roll(u, 1, 0))
        u2 = jnp.where(rows == 0, hist[6:7], jnp.where(rows == 1, hist[7:8], pltpu.roll(u, 2, 0)))
        carry_ref[...] = u[tm - SUBLANES:]
        utail_ref[...] = u[tm - SUBLANES:]
    else:
        t = rows % seq_rows
        prev = prev_ref[...]
        u1 = jnp.where(t == 0, pltpu.roll(prev, tm - (seq_rows - 1), 0), pltpu.roll(u, 1, 0))
        u2 = jnp.where(t < 2, pltpu.roll(prev, tm - (seq_rows - 2), 0), pltpu.roll(u, 2, 0))
        utail_ref[...] = u
    wconv = wconv_ref[...]
    conv = u2 * wconv[0:1] + u1 * wconv[1:2] + u * wconv[2:3]
    cbz_ref[...] = (cb * conv).astype(BF16)

    gates = jnp.dot(xn, wg_ref[...], preferred_element_type=F32)
    sga_ref[...] = jax.nn.sigmoid(gates[:, :D_MODEL])
    sgb_ref[...] = jax.nn.sigmoid(gates[:, D_MODEL:])


def _prep_in_weights(w_in):
    o_f = 3 * FOX_WIDTH
    o_c = o_f + FOX_HEADS
    o_g = o_c + 3 * CONV_WIDTH
    wqkv = w_in[:, :o_f].astype(BF16)
    wf = jnp.pad(w_in[:, o_f:o_c], ((0, 0), (0, LANES - FOX_HEADS))).astype(BF16)
    wc = w_in[:, o_c:o_g].astype(BF16)
    wg = w_in[:, o_g:].astype(BF16)
    return wqkv, wf, wc, wg


def _in_proj(x2, g_mix, wqkv, wf, wc, wg, b_forget, w_conv, prev8, *, n_seq, seq_len, tm):
    n = x2.shape[0]
    if seq_len >= tm:
        assert seq_len % tm == 0
        nt = seq_len // tm
        grid = (n_seq, nt)
        tok = lambda w: pl.BlockSpec((tm, w), lambda b, t: (b * nt + t, 0))
        seq8 = pl.BlockSpec((SUBLANES, CONV_WIDTH), lambda b, t: (b, 0))
        seq_rows = None
    else:
        assert seq_len == SUBLANES and tm == n
        grid = (1, 1)
        tok = lambda w: pl.BlockSpec((tm, w), lambda b, t: (0, 0))
        seq8 = pl.BlockSpec((tm, CONV_WIDTH), lambda b, t: (0, 0))
        seq_rows = seq_len
    out_shape = (
        jax.ShapeDtypeStruct((n, FOX_WIDTH), BF16),
        jax.ShapeDtypeStruct((n, FOX_WIDTH), F32),
        jax.ShapeDtypeStruct((n, FOX_WIDTH), F32),
        jax.ShapeDtypeStruct((n, FOX_WIDTH), BF16),
        jax.ShapeDtypeStruct((n, FOX_WIDTH), BF16),
        jax.ShapeDtypeStruct((n, FOX_HEADS), F32),
        jax.ShapeDtypeStruct((n, CONV_WIDTH), BF16),
        jax.ShapeDtypeStruct((n, D_MODEL), F32),
        jax.ShapeDtypeStruct((n, D_MODEL), F32),
        jax.ShapeDtypeStruct((n_seq * SUBLANES, CONV_WIDTH), F32),
    )
    out_specs = (tok(FOX_WIDTH), tok(FOX_WIDTH), tok(FOX_WIDTH), tok(FOX_WIDTH), tok(FOX_WIDTH),
                 tok(FOX_HEADS), tok(CONV_WIDTH), tok(D_MODEL), tok(D_MODEL), seq8)
    in_specs = [tok(D_MODEL), _full((1, D_MODEL)), _full(wqkv.shape), _full(wf.shape), _full(wc.shape),
                _full(wg.shape), _full((1, FOX_HEADS)), _full((CONV_K, CONV_WIDTH)), seq8]
    return pl.pallas_call(
        functools.partial(_in_proj_body, seq_rows=seq_rows),
        grid=grid, in_specs=in_specs, out_specs=out_specs, out_shape=out_shape,
        scratch_shapes=[pltpu.VMEM((SUBLANES, CONV_WIDTH), F32)],
        compiler_params=_cparams(("arbitrary", "arbitrary")),
        name="in_proj",
    )(x2, g_mix.reshape(1, D_MODEL), wqkv, wf, wc, wg, b_forget.reshape(1, FOX_HEADS), w_conv, prev8)


def _lane_prefix_scan(x, lane):
    s = x
    sh = 1
    while sh < LANES:
        s = s + jnp.where(lane >= sh, pltpu.roll(s, sh, 1), 0.0)
        sh *= 2
    return s


def _cumsum_body(x_ref, o_ref):
    n_chunks = x_ref.shape[2] // LANES
    lane = lax.broadcasted_iota(I32, (FOX_HEADS, LANES), 1)

    def step(c, carry):
        off = pl.multiple_of(c * LANES, LANES)
        s = _lane_prefix_scan(x_ref[0, :, pl.ds(off, LANES)], lane) + carry
        o_ref[0, :, pl.ds(off, LANES)] = s
        return s[:, LANES - 1:]

    lax.fori_loop(0, n_chunks, step, jnp.zeros((FOX_HEADS, 1), F32))


def _cumsum_time(logf_t):
    b, h, t = logf_t.shape
    spec = pl.BlockSpec((1, h, t), lambda i: (i, 0, 0))
    return pl.pallas_call(
        _cumsum_body, grid=(b,), in_specs=[spec], out_specs=spec,
        out_shape=jax.ShapeDtypeStruct(logf_t.shape, F32),
        compiler_params=_cparams(("arbitrary",)), name="forget_cumsum",
    )(logf_t)


def _fox_prompt_body(q_ref, k_ref, v_ref, fcol_ref, frow_ref, o_ref, *, tq):
    hp = pl.program_id(1)
    qi = pl.program_id(2)
    q2 = q_ref[0]
    lane = lax.broadcasted_iota(I32, (tq, LANES), 1)
    upper = lane >= FOX_HEAD_DIM
    col8 = lax.broadcasted_iota(I32, (tq, FOX_HEADS), 1)
    fcol8 = fcol_ref[0]
    rowi = lax.broadcasted_iota(I32, (tq, tq), 0)
    coli = lax.broadcasted_iota(I32, (tq, tq), 1)
    zero = jnp.zeros_like(q2)

    outs = []
    for hh in range(2):
        h = 2 * hp + hh
        qh = jnp.where(upper, q2, zero) if hh else jnp.where(upper, zero, q2)
        fq = jnp.sum(jnp.where(col8 == h, fcol8, 0.0), axis=1, keepdims=True)

        def block(kb, carry, masked):
            m, l, acc = carry
            off = pl.multiple_of(kb * tq, tq)
            kk = k_ref[0, pl.ds(off, tq), :]
            vv = v_ref[0, pl.ds(off, tq), :]
            fk = frow_ref[0, pl.ds(h, 1), pl.ds(off, tq)]
            s = lax.dot_general(qh, kk, (((1,), (1,)), ((), ())), preferred_element_type=F32)
            s = s + fq - fk
            if masked:
                s = jnp.where(coli <= rowi, s, NEG_INF)
            m_new = jnp.maximum(m, jnp.max(s, axis=1, keepdims=True))
            p = jnp.exp(s - m_new)
            alpha = jnp.exp(m - m_new)
            l = alpha * l + jnp.sum(p, axis=1, keepdims=True)
            acc = alpha * acc + jnp.dot(p.astype(BF16), vv, preferred_element_type=F32)
            return m_new, l, acc

        init = (jnp.full((tq, 1), NEG_INF, F32), jnp.zeros((tq, 1), F32), jnp.zeros((tq, LANES), F32))
        carry = lax.fori_loop(0, qi, functools.partial(block, masked=False), init)
        m, l, acc = block(qi, carry, True)
        outs.append(acc / l)
    o_ref[0] = jnp.where(upper, outs[1], outs[0]).astype(o_ref.dtype)


def _fox_prompt(q, k, v, fcol, frow, *, tq):
    b, t, _ = q.shape
    n_pairs = FOX_WIDTH // LANES
    return pl.pallas_call(
        functools.partial(_fox_prompt_body, tq=tq),
        grid=(b, n_pairs, t // tq),
        in_specs=[
            pl.BlockSpec((1, tq, LANES), lambda i, p, j: (i, j, p)),
            pl.BlockSpec((1, t, LANES), lambda i, p, j: (i, 0, p)),
            pl.BlockSpec((1, t, LANES), lambda i, p, j: (i, 0, p)),
            pl.BlockSpec((1, tq, FOX_HEADS), lambda i, p, j: (i, j, 0)),
            pl.BlockSpec((1, FOX_HEADS, t), lambda i, p, j: (i, 0, 0)),
        ],
        out_specs=pl.BlockSpec((1, tq, LANES), lambda i, p, j: (i, j, p)),
        out_shape=jax.ShapeDtypeStruct((b, t, FOX_WIDTH), BF16),
        compiler_params=_cparams(("arbitrary", "arbitrary", "arbitrary")),
        name="fox_prompt",
    )(q, k, v, fcol, frow)


SAMPLE_ROWS = 64


def _lane_suffix_excl(x, lane):
    s = jnp.where(lane < LANES - 1, pltpu.roll(x, LANES - 1, 1), 0.0)
    sh = 1
    while sh < LANES:
        s = s + jnp.where(lane < LANES - sh, pltpu.roll(s, LANES - sh, 1), 0.0)
        sh *= 2
    return s


def _fox_sample_body(pt_ref, q_ref, kn_ref, vn_ref, lfcol_ref, lfrow_ref, *rest, pages_per_step):
    pg = pages_per_step
    k_refs = rest[:pg]
    v_refs = rest[pg:2 * pg]
    lf_refs = rest[2 * pg:3 * pg]
    o_ref = rest[3 * pg]
    m_ref, l_ref, acc_ref, tot_ref, qbd_ref, rt_ref = rest[3 * pg + 1:]
    g = pl.program_id(1)
    lane8 = lax.broadcasted_iota(I32, (FOX_HEADS, LANES), 1)
    row = lax.broadcasted_iota(I32, (SAMPLE_ROWS, FOX_WIDTH), 0)
    lane = lax.broadcasted_iota(I32, (SAMPLE_ROWS, FOX_WIDTH), 1)
    head_mask = (lane // FOX_HEAD_DIM) == (row % FOX_HEADS)
    nt = (((1,), (1,)), ((), ()))

    def tile8(x):
        return jnp.concatenate([x] * (SAMPLE_ROWS // FOX_HEADS), axis=0)

    @pl.when(g == 0)
    def _():
        q = q_ref[0]
        qrows = jnp.concatenate(
            [jnp.broadcast_to(q[t:t + 1, :], (FOX_HEADS, FOX_WIDTH)) for t in range(q.shape[0])], axis=0)
        qbd = jnp.where(head_mask, qrows, jnp.zeros_like(qrows))
        qbd_ref[...] = qbd
        r1 = lax.broadcasted_iota(I32, (SAMPLE_ROWS, 1), 0)
        x = lfcol_ref[0]
        s = jnp.where(r1 < SAMPLE_ROWS - 8, pltpu.roll(x, SAMPLE_ROWS - 8, 0), 0.0)
        for sh in (8, 16, 32):
            s = s + jnp.where(r1 < SAMPLE_ROWS - sh, pltpu.roll(s, SAMPLE_ROWS - sh, 0), 0.0)
        rt = -s
        rt_ref[...] = rt
        lfn = lfrow_ref[0]
        excl = _lane_suffix_excl(lfn, lane8)
        tot_ref[...] = excl[:, 0:1] + lfn[:, 0:1]
        sc = jnp.dot(qbd, kn_ref[0], preferred_element_type=F32) + rt + tile8(excl)
        rr = lax.broadcasted_iota(I32, (SAMPLE_ROWS, LANES), 0)
        cc = lax.broadcasted_iota(I32, (SAMPLE_ROWS, LANES), 1)
        sc = jnp.where(cc <= rr // FOX_HEADS, sc, NEG_INF)
        m_new = jnp.max(sc, axis=1, keepdims=True)
        p = jnp.exp(sc - m_new)
        m_ref[...] = m_new
        l_ref[...] = jnp.sum(p, axis=1, keepdims=True)
        acc_ref[...] = lax.dot_general(p.astype(BF16), vn_ref[0], nt, preferred_element_type=F32)

    qbd = qbd_ref[...]
    tot = tot_ref[...]
    biases, ks, vs = [], [], []
    for i in reversed(range(pg)):
        lfp = lf_refs[i][0]
        r = _lane_suffix_excl(lfp, lane8) + tot
        tot = r[:, 0:1] + lfp[:, 0:1]
        biases.append(tile8(r))
        ks.append(k_refs[i][0].reshape(FOX_WIDTH, PAGE_SIZE).astype(BF16))
        vs.append(v_refs[i][0].reshape(FOX_WIDTH, PAGE_SIZE).astype(BF16))
    tot_ref[...] = tot
    sc = jnp.dot(qbd, jnp.concatenate(ks, axis=1), preferred_element_type=F32)
    sc = sc + rt_ref[...] + jnp.concatenate(biases, axis=1)
    m_old = m_ref[...]
    m_new = jnp.maximum(m_old, jnp.max(sc, axis=1, keepdims=True))
    p = jnp.exp(sc - m_new)
    alpha = jnp.exp(m_old - m_new)
    l_ref[...] = alpha * l_ref[...] + jnp.sum(p, axis=1, keepdims=True)
    acc_ref[...] = alpha * acc_ref[...] + lax.dot_general(
        p.astype(BF16), jnp.concatenate(vs, axis=1), nt, preferred_element_type=F32)
    m_ref[...] = m_new

    @pl.when(g == pl.num_programs(1) - 1)
    def _():
        full = jnp.where(head_mask, acc_ref[...] / l_ref[...], 0.0)
        o_ref[0] = jnp.concatenate(
            [jnp.sum(full[FOX_HEADS * t:FOX_HEADS * (t + 1)], axis=0, keepdims=True)
             for t in range(SAMPLE_ROWS // FOX_HEADS)], axis=0).astype(o_ref.dtype)


def _fox_sample(page_table, q, kn_t, vn_t, lfcol, lfrow_pad, cache_kt, cache_vt, cache_lf_t, *, pages_per_step):
    bd, n_pages = page_table.shape
    pg = pages_per_step
    assert n_pages % pg == 0
    ng = n_pages // pg
    s_new = q.shape[1]

    def page_spec(shape, i):
        zeros = (0,) * (len(shape) - 1)
        return pl.BlockSpec(shape, lambda b, g, pt: (pt[b, (ng - 1 - g) * pg + i],) + zeros)

    req = lambda shape: pl.BlockSpec(shape, lambda b, g, pt: (b, 0, 0))
    in_specs = [req((1, s_new, FOX_WIDTH)), req((1, FOX_WIDTH, PAGE_SIZE)), req((1, FOX_WIDTH, PAGE_SIZE)),
                req((1, SAMPLE_ROWS, 1)), req((1, FOX_HEADS, LANES))]
    in_specs += [page_spec((1, FOX_HEADS, FOX_HEAD_DIM, PAGE_SIZE), i) for i in range(pg)]
    in_specs += [page_spec((1, FOX_HEADS, FOX_HEAD_DIM, PAGE_SIZE), i) for i in range(pg)]
    in_specs += [page_spec((1, FOX_HEADS, PAGE_SIZE), i) for i in range(pg)]
    grid_spec = pltpu.PrefetchScalarGridSpec(
        num_scalar_prefetch=1, grid=(bd, ng), in_specs=in_specs,
        out_specs=req((1, s_new, FOX_WIDTH)),
        scratch_shapes=[pltpu.VMEM((SAMPLE_ROWS, 1), F32), pltpu.VMEM((SAMPLE_ROWS, 1), F32),
                        pltpu.VMEM((SAMPLE_ROWS, FOX_WIDTH), F32), pltpu.VMEM((FOX_HEADS, 1), F32),
                        pltpu.VMEM((SAMPLE_ROWS, FOX_WIDTH), BF16), pltpu.VMEM((SAMPLE_ROWS, 1), F32)])
    return pl.pallas_call(
        functools.partial(_fox_sample_body, pages_per_step=pg),
        grid_spec=grid_spec, out_shape=jax.ShapeDtypeStruct((bd, s_new, FOX_WIDTH), BF16),
        compiler_params=_cparams(("arbitrary", "arbitrary")), name="fox_sample",
    )(page_table, q, kn_t, vn_t, lfcol, lfrow_pad, *([cache_kt] * pg), *([cache_vt] * pg), *([cache_lf_t] * pg))


def _merge_body(x_ref, attn_ref, cbz_ref, sga_ref, sgb_ref, wfo_ref, wco_ref, wo_ref, gx_ref, wxq_ref,
                h1_ref, qx_ref):
    ya = jnp.dot(attn_ref[...], wfo_ref[...], preferred_element_type=F32)
    yb = jnp.dot(cbz_ref[...], wco_ref[...], preferred_element_type=F32)
    mix = (sga_ref[...] * ya + sgb_ref[...] * yb).astype(BF16)
    h1 = x_ref[...] + jnp.dot(mix, wo_ref[...], preferred_element_type=F32)
    h1_ref[...] = h1
    xn = _rmsnorm(h1, gx_ref[...]).astype(BF16)
    qx_ref[...] = (jnp.dot(xn, wxq_ref[...], preferred_element_type=F32) * (XA_HEAD_DIM ** -0.5)).astype(BF16)


def _merge(x2, attn, cbz, sga, sgb, wfo, wco, wo, g_xattn, wxq, *, tm):
    n = x2.shape[0]
    tok = lambda w: pl.BlockSpec((tm, w), lambda i: (i, 0))
    return pl.pallas_call(
        _merge_body, grid=(n // tm,),
        in_specs=[tok(D_MODEL), tok(FOX_WIDTH), tok(CONV_WIDTH), tok(D_MODEL), tok(D_MODEL),
                  _full(wfo.shape), _full(wco.shape), _full(wo.shape), _full((1, D_MODEL)), _full(wxq.shape)],
        out_specs=(tok(D_MODEL), tok(D_MODEL)),
        out_shape=(jax.ShapeDtypeStruct((n, D_MODEL), F32), jax.ShapeDtypeStruct((n, D_MODEL), BF16)),
        compiler_params=_cparams(("arbitrary",)), name="merge",
    )(x2, attn, cbz, sga, sgb, wfo, wco, wo, g_xattn.reshape(1, D_MODEL), wxq)


def _memory_kv_body(m_ref, g_ref, wk_ref, wv_ref, mk_ref, mv_ref):
    mn = _rmsnorm(m_ref[...], g_ref[...]).astype(BF16)
    mk_ref[...] = jnp.dot(mn, wk_ref[...], preferred_element_type=F32)
    mv_ref[...] = jnp.dot(mn, wv_ref[...], preferred_element_type=F32)


def _memory_kv(mem2, g_mem, wxk, wxv, *, tm):
    n = mem2.shape[0]
    tok = pl.BlockSpec((tm, D_MODEL), lambda i: (i, 0))
    return pl.pallas_call(
        _memory_kv_body, grid=(n // tm,),
        in_specs=[tok, _full((1, D_MODEL)), _full(wxk.shape), _full(wxv.shape)],
        out_specs=(tok, tok),
        out_shape=(jax.ShapeDtypeStruct((n, D_MODEL), F32),) * 2,
        compiler_params=_cparams(("arbitrary",)), name="memory_kv",
    )(mem2, g_mem.reshape(1, D_MODEL), wxk, wxv)


def _cross_attn_body(q_ref, mk_ref, mv_ref, o_ref):
    q = q_ref[0]
    outs = []
    for h in range(XA_HEADS):
        sl = slice(h * XA_HEAD_DIM, (h + 1) * XA_HEAD_DIM)
        kh = mk_ref[0, :, sl].astype(BF16)
        vh = mv_ref[0, :, sl].astype(BF16)
        s = lax.dot_general(q[:, sl], kh, (((1,), (1,)), ((), ())), preferred_element_type=F32)
        p = jnp.exp(s - jnp.max(s, axis=1, keepdims=True))
        p = p / jnp.sum(p, axis=1, keepdims=True)
        outs.append(jnp.dot(p.astype(BF16), vh, preferred_element_type=F32))
    o_ref[0] = jnp.concatenate(outs, axis=1).astype(o_ref.dtype)


def _cross_attn(qx, mk, mv, *, tm):
    b, t, _ = qx.shape
    m = mk.shape[1]
    tok = pl.BlockSpec((1, tm, D_MODEL), lambda i, j: (i, j, 0))
    mem = pl.BlockSpec((1, m, D_MODEL), lambda i, j: (i, 0, 0))
    return pl.pallas_call(
        _cross_attn_body, grid=(b, t // tm), in_specs=[tok, mem, mem], out_specs=tok,
        out_shape=jax.ShapeDtypeStruct((b, t, D_MODEL), BF16),
        compiler_params=_cparams(("arbitrary", "arbitrary")), name="cross_attn",
    )(qx, mk, mv)


PEER_PAIRS = tuple((i, j) for i in range(PEER_TOPK) for j in range(PEER_TOPK) if (i + 1) * (j + 1) <= PEER_TOPK)
PEER_CAND_ROWS = -(-len(PEER_PAIRS) // SUBLANES) * SUBLANES


def _top_rows(s, k, payload=None):
    n = s.shape[0]
    iota = lax.broadcasted_iota(I32, s.shape, 0)
    vals, ids = [], []
    for _ in range(k):
        m = jnp.max(s, axis=0, keepdims=True)
        r = jnp.min(jnp.where(s == m, iota, n), axis=0, keepdims=True)
        hit = iota == r
        vals.append(m)
        ids.append(r if payload is None else jnp.max(jnp.where(hit, payload, -1), axis=0, keepdims=True))
        s = jnp.where(hit, NEG_INF, s)
    return vals, ids


def _peer_front_body(h1_ref, o_ref, wxo_ref, gp_ref, wpq_ref, ka_ref, kb_ref,
                     h2_ref, xn_ref, eidx_ref, etok_ref, gate_ref, qp_ref, e_ref, g_ref):
    tm = h1_ref.shape[0]
    h2 = h1_ref[...] + jnp.dot(o_ref[...], wxo_ref[...], preferred_element_type=F32)
    h2_ref[...] = h2
    xn = _rmsnorm(h2, gp_ref[...]).astype(BF16)
    xn_ref[...] = xn.astype(F32)
    qp_ref[...] = jnp.dot(xn, wpq_ref[...], preferred_element_type=F32).astype(BF16)

    def head(h, _):
        qh = qp_ref[:, pl.ds(pl.multiple_of(h * LANES, LANES), LANES)]
        nt = (((1,), (1,)), ((), ()))
        sa = lax.dot_general(ka_ref[h], qh, nt, preferred_element_type=F32)
        sb = lax.dot_general(kb_ref[h], qh, nt, preferred_element_type=F32)
        va, ia = _top_rows(sa, PEER_TOPK)
        vb, ib = _top_rows(sb, PEER_TOPK)
        pad = PEER_CAND_ROWS - len(PEER_PAIRS)
        cand = jnp.concatenate([va[i] + vb[j] for i, j in PEER_PAIRS]
                               + [jnp.full((pad, tm), NEG_INF, F32)], axis=0)
        ceid = jnp.concatenate([ia[i] * PEER_N_KEYS + ib[j] for i, j in PEER_PAIRS]
                               + [jnp.zeros((pad, tm), I32)], axis=0)
        ts, te = _top_rows(cand, PEER_TOPK, payload=ceid)
        ex = [jnp.exp(t - ts[0]) for t in ts]
        den = ex[0]
        for e in ex[1:]:
            den = den + e
        row0 = pl.multiple_of(h * PEER_TOPK, PEER_TOPK)
        e_ref[pl.ds(row0, PEER_TOPK), :] = jnp.concatenate(te, axis=0)
        g_ref[pl.ds(row0, PEER_TOPK), :] = jnp.concatenate([e / den for e in ex], axis=0)
        return 0

    lax.fori_loop(0, PEER_HEADS, head, 0)
    for c in range(eidx_ref.shape[0]):
        eidx_ref[c] = e_ref[:, c * PEER_TILE:(c + 1) * PEER_TILE]
    etok_ref[...] = e_ref[...].T
    gate_ref[...] = g_ref[...].T


def _prep_peer_keys(keys_a, keys_b):
    z = jnp.zeros_like(keys_a)
    return (jnp.concatenate([keys_a, z], axis=-1).astype(BF16), jnp.concatenate([z, keys_b], axis=-1).astype(BF16))


def _peer_front(h1, o, wxo, g_peer, wpq, ka_pad, kb_pad, *, tm):
    n = h1.shape[0]
    assert tm % PEER_TILE == 0
    sub = tm // PEER_TILE
    tok = lambda w: pl.BlockSpec((tm, w), lambda i: (i, 0))
    return pl.pallas_call(
        _peer_front_body, grid=(n // tm,),
        in_specs=[tok(D_MODEL), tok(D_MODEL), _full(wxo.shape), _full((1, D_MODEL)), _full(wpq.shape),
                  _full(ka_pad.shape), _full(kb_pad.shape)],
        out_specs=(tok(D_MODEL), tok(D_MODEL), pl.BlockSpec((sub, PEER_SLOTS, PEER_TILE), lambda i: (i, 0, 0)),
                   tok(PEER_SLOTS), tok(PEER_SLOTS)),
        out_shape=(jax.ShapeDtypeStruct((n, D_MODEL), F32), jax.ShapeDtypeStruct((n, D_MODEL), F32),
                   jax.ShapeDtypeStruct((n // PEER_TILE, PEER_SLOTS, PEER_TILE), I32),
                   jax.ShapeDtypeStruct((n, PEER_SLOTS), I32),
                   jax.ShapeDtypeStruct((n, PEER_SLOTS), F32)),
        scratch_shapes=[pltpu.VMEM((tm, PEER_HEADS * LANES), BF16), pltpu.VMEM((PEER_SLOTS, tm), I32),
                        pltpu.VMEM((PEER_SLOTS, tm), F32)],
        compiler_params=_cparams(("arbitrary",)), name="peer_front",
    )(h1, o, wxo, g_peer.reshape(1, D_MODEL), wpq, ka_pad, kb_pad)


PACKED_WIDTH = D_MODEL // 2


def _pack_table(t):
    b = lax.bitcast_convert_type(t.astype(BF16), jnp.uint16).astype(jnp.uint32)
    return lax.bitcast_convert_type(b[:, :PACKED_WIDTH] | (b[:, PACKED_WIDTH:] << 16), I32)


def _unpack(w):
    lo = lax.bitcast_convert_type(w << 16, F32)
    hi = lax.bitcast_convert_type(w & jnp.int32(-65536), F32)
    return lo, hi


SC_CORES = 2
SC_SUBCORES = 16
SC_WINDOW = 64


def _sc_gather2(table_u, table_v, idx):
    r = idx.shape[0]
    workers = SC_CORES * SC_SUBCORES
    assert r % (workers * SC_WINDOW) == 0
    per_worker = r // workers
    n_win = per_worker // SC_WINDOW
    width = table_u.shape[1]
    mesh = plsc.VectorSubcoreMesh(core_axis_name="c", subcore_axis_name="s")
    out = jax.ShapeDtypeStruct((r, width), table_u.dtype)

    @functools.partial(
        pl.kernel, mesh=mesh, out_type=(out, out),
        scratch_types=[pltpu.VMEM((SC_WINDOW,), I32), pltpu.VMEM((SC_WINDOW, width), table_u.dtype),
                       pltpu.VMEM((SC_WINDOW, width), table_u.dtype),
                       pltpu.SemaphoreType.DMA, pltpu.SemaphoreType.DMA])
    def gather(u_hbm, v_hbm, idx_hbm, ou_hbm, ov_hbm, idx_v, ru_v, rv_v, sem_u, sem_v):
        wid = lax.axis_index("s") * SC_CORES + lax.axis_index("c")
        base = wid * per_worker

        @pl.loop(0, n_win)
        def _(i):
            off = base + i * SC_WINDOW
            pltpu.sync_copy(idx_hbm.at[pl.ds(off, SC_WINDOW)], idx_v)
            cu = pltpu.async_copy(u_hbm.at[idx_v], ru_v, sem_u)
            cv = pltpu.async_copy(v_hbm.at[idx_v], rv_v, sem_v)
            cu.wait()
            pltpu.sync_copy(ru_v, ou_hbm.at[pl.ds(off, SC_WINDOW)])
            cv.wait()
            pltpu.sync_copy(rv_v, ov_hbm.at[pl.ds(off, SC_WINDOW)])

    return gather(table_u, table_v, idx)


def _peer_back_body(ug_ref, vg_ref, xn_ref, gate_ref, h2_ref, gf_ref, y_ref, x32_ref, out_ref, *, slots_per_step):
    sg = pl.program_id(1)
    tp = xn_ref.shape[0]
    n_groups = tp // SUBLANES
    lane = lax.broadcasted_iota(I32, (SUBLANES, PEER_SLOTS), 1)
    slot0 = sg * slots_per_step

    @pl.when(sg == 0)
    def _():
        x32_ref[...] = xn_ref[...].astype(F32)
        out_ref[...] = jnp.zeros_like(out_ref)

    def rows_of(r):
        return pl.ds(pl.multiple_of(r * SUBLANES, SUBLANES), SUBLANES)

    def u_pass(r):
        rows = rows_of(r)
        xlo = x32_ref[rows, :PACKED_WIDTH]
        xhi = x32_ref[rows, PACKED_WIDTH:]
        act = jnp.zeros((SUBLANES, PEER_SLOTS), F32)
        for j in range(slots_per_step):
            ulo, uhi = _unpack(ug_ref[0, j, rows, :])
            a = jnp.sum(ulo * xlo + uhi * xhi, axis=1, keepdims=True)
            act = jnp.where(lane == slot0 + j, a, act)
        return gate_ref[rows, :] * jax.nn.gelu(act)

    def v_pass(r, wts):
        rows = rows_of(r)
        olo = out_ref[rows, :PACKED_WIDTH]
        ohi = out_ref[rows, PACKED_WIDTH:]
        for j in range(slots_per_step):
            w = jnp.sum(jnp.where(lane == slot0 + j, wts, 0.0), axis=1, keepdims=True)
            vlo, vhi = _unpack(vg_ref[0, j, rows, :])
            olo = olo + w * vlo
            ohi = ohi + w * vhi
        out_ref[rows, :PACKED_WIDTH] = olo
        out_ref[rows, PACKED_WIDTH:] = ohi

    def group(r, wts):
        nxt = u_pass(r)
        v_pass(r - 1, wts)
        return nxt

    last = lax.fori_loop(1, n_groups, group, u_pass(0))
    v_pass(n_groups - 1, last)

    @pl.when(sg == pl.num_programs(1) - 1)
    def _():
        y_ref[...] = _rmsnorm(h2_ref[...] + out_ref[...], gf_ref[...])


SC_ROWS = 16
SC_LANES = 16
PART_WIDTH = PEER_SLOTS * SC_LANES


def _sc_peer(table_u, table_v, idx_v, idx_u, x32):
    n = x32.shape[0]
    workers = SC_CORES * SC_SUBCORES
    tw = n // workers
    assert n % (2 * workers) == 0
    wins = PEER_SLOTS // SC_ROWS
    words = SC_ROWS * SC_LANES
    n_col = PACKED_WIDTH // SC_LANES
    mesh = plsc.VectorSubcoreMesh(core_axis_name="c", subcore_axis_name="s")
    dma = pltpu.SemaphoreType.DMA

    @functools.partial(
        pl.kernel, mesh=mesh, compiler_params=pltpu.CompilerParams(needs_layout_passes=False),
        out_type=(jax.ShapeDtypeStruct((n * PEER_SLOTS, PACKED_WIDTH), I32),
                  jax.ShapeDtypeStruct((n, PART_WIDTH), F32)),
        scratch_types=[pltpu.VMEM((tw * PEER_SLOTS,), I32), pltpu.VMEM((tw * PEER_SLOTS,), I32),
                       pltpu.VMEM((2, SC_ROWS), I32), pltpu.VMEM((2, SC_ROWS), I32),
                       pltpu.VMEM((2, SC_ROWS, PACKED_WIDTH), I32), pltpu.VMEM((2, SC_ROWS, PACKED_WIDTH), I32),
                       pltpu.VMEM((2, D_MODEL), F32), pltpu.VMEM((2, PART_WIDTH), F32)] + [dma] * 10)
    def body(u_hbm, v_hbm, idxv_hbm, idxu_hbm, x_hbm, ov_hbm, part_hbm,
             idxu_all, idxv_all, iu, iv, ru, rv, xv, pv, su0, su1, sv0, sv1, sw0, sw1, sx0, sx1, sp0, sp1):
        su, sv, sw, sx, sp = (su0, su1), (sv0, sv1), (sw0, sw1), (sx0, sx1), (sp0, sp1)
        wid = lax.axis_index("s") * SC_CORES + lax.axis_index("c")
        tok0 = wid * tw
        row0 = tok0 * PEER_SLOTS
        pltpu.sync_copy(idxu_hbm.at[pl.ds(row0, tw * PEER_SLOTS)], idxu_all)
        pltpu.sync_copy(idxv_hbm.at[pl.ds(row0, tw * PEER_SLOTS)], idxv_all)

        def u_gather(b):
            return pltpu.make_async_copy(u_hbm.at[iu.at[b]], ru.at[b], su[b])

        def v_gather(b):
            return pltpu.make_async_copy(v_hbm.at[iv.at[b]], rv.at[b], sv[b])

        def v_write(b, k):
            return pltpu.make_async_copy(rv.at[b], ov_hbm.at[pl.ds(row0 + k * SC_ROWS, SC_ROWS)], sw[b])

        def x_load(par, i):
            return pltpu.make_async_copy(x_hbm.at[tok0 + i], xv.at[par], sx[par])

        def part_write(par, i):
            return pltpu.make_async_copy(pv.at[par], part_hbm.at[tok0 + i], sp[par])

        def start_window(b, k):
            off = pl.multiple_of(k * SC_ROWS, SC_ROWS)
            iu[b, :] = idxu_all[pl.ds(off, SC_ROWS)]
            iv[b, :] = idxv_all[pl.ds(off, SC_ROWS)]
            u_gather(b).start()
            v_gather(b).start()

        def dots(b, par, w):
            for s in range(SC_ROWS):
                pv[par, pl.ds(w * words + s * SC_LANES, SC_LANES)] = jnp.zeros((SC_LANES,), F32)

            @pl.loop(0, n_col)
            def _(j):
                c = pl.multiple_of(j * SC_LANES, SC_LANES)
                xlo = xv[par, pl.ds(c, SC_LANES)]
                xhi = xv[par, pl.ds(PACKED_WIDTH + c, SC_LANES)]
                rows = [ru[b, s, pl.ds(c, SC_LANES)] for s in range(SC_ROWS)]
                vals = [plsc.bitcast(r << 16, F32) * xlo + plsc.bitcast(r & jnp.int32(-65536), F32) * xhi
                        for r in rows]
                for s in range(SC_ROWS):
                    plsc.addupdate(pv.at[par, pl.ds(w * words + s * SC_LANES, SC_LANES)], vals[s])

        x_load(0, 0).start()
        start_window(0, 0)

        @pl.loop(0, tw // 2)
        def _(i2):
            for par in range(2):
                i = 2 * i2 + par
                more_tokens = i2 < tw // 2 - 1 if par else None
                if par == 0:
                    x_load(1, i + 1).start()
                else:
                    @pl.when(more_tokens)
                    def _():
                        x_load(0, i + 1).start()
                x_load(par, i).wait()

                @pl.when(i2 >= 1)
                def _():
                    part_write(par, i).wait()

                for w in range(wins):
                    b, nb = w % 2, 1 - w % 2
                    k = i * wins + w
                    if w >= 1 or par == 1:
                        v_write(nb, k).wait()
                    else:
                        @pl.when(i2 >= 1)
                        def _():
                            v_write(nb, k).wait()
                    if w < wins - 1 or par == 0:
                        start_window(nb, k + 1)
                    else:
                        @pl.when(more_tokens)
                        def _():
                            start_window(nb, k + 1)
                    u_gather(b).wait()
                    dots(b, par, w)
                    v_gather(b).wait()
                    v_write(b, k).start()
                part_write(par, i).start()

        v_write(1, 0).wait()
        part_write(0, 0).wait()
        part_write(1, 0).wait()

    return body(table_u, table_v, idx_v, idx_u, x32)


def _peer_back2_body(vg_ref, part_ref, ones_ref, gate_ref, h2_ref, gf_ref, y_ref, wts_ref, out_ref, *, slots_per_step):
    sg = pl.program_id(1)
    tp = gate_ref.shape[0]
    lane = lax.broadcasted_iota(I32, (SUBLANES, PEER_SLOTS), 1)
    slot0 = sg * slots_per_step

    @pl.when(sg == 0)
    def _():
        act = jnp.dot(part_ref[...], ones_ref[...], precision=lax.Precision.HIGHEST, preferred_element_type=F32)
        wts_ref[...] = gate_ref[...] * jax.nn.gelu(act)
        out_ref[...] = jnp.zeros_like(out_ref)

    def group(r, _):
        rows = pl.ds(pl.multiple_of(r * SUBLANES, SUBLANES), SUBLANES)
        wts = wts_ref[rows, :]
        olo = out_ref[rows, :PACKED_WIDTH]
        ohi = out_ref[rows, PACKED_WIDTH:]
        for j in range(slots_per_step):
            w = jnp.sum(jnp.where(lane == slot0 + j, wts, 0.0), axis=1, keepdims=True)
            vlo, vhi = _unpack(vg_ref[0, j, rows, :])
            olo = olo + w * vlo
            ohi = ohi + w * vhi
        out_ref[rows, :PACKED_WIDTH] = olo
        out_ref[rows, PACKED_WIDTH:] = ohi
        return 0

    lax.fori_loop(0, tp // SUBLANES, group, 0)

    @pl.when(sg == pl.num_programs(1) - 1)
    def _():
        y_ref[...] = _rmsnorm(h2_ref[...] + out_ref[...], gf_ref[...])


def _peer_back2(vg, part, gate, h2, g_final, *, tp, slots_per_step):
    n = gate.shape[0]
    n_sg = PEER_SLOTS // slots_per_step
    ones = (jnp.arange(PART_WIDTH)[:, None] // SC_LANES == jnp.arange(PEER_SLOTS)[None, :]).astype(F32)
    rows = pl.BlockSpec((1, slots_per_step, tp, PACKED_WIDTH), lambda i, s: (i, s, 0, 0))
    tok = lambda w: pl.BlockSpec((tp, w), lambda i, s: (i, 0))
    return pl.pallas_call(
        functools.partial(_peer_back2_body, slots_per_step=slots_per_step),
        grid=(n // tp, n_sg),
        in_specs=[rows, tok(PART_WIDTH), _full(ones.shape), tok(PEER_SLOTS), tok(D_MODEL), _full((1, D_MODEL))],
        out_specs=tok(D_MODEL),
        out_shape=jax.ShapeDtypeStruct((n, D_MODEL), F32),
        scratch_shapes=[pltpu.VMEM((tp, PEER_SLOTS), F32), pltpu.VMEM((tp, D_MODEL), F32)],
        compiler_params=_cparams(("arbitrary", "arbitrary")), name="peer_back",
    )(vg, part, ones, gate, h2, g_final.reshape(1, D_MODEL))


def _peer_back(ug, vg, xn, gate, h2, g_final, *, tp, slots_per_step):
    n = xn.shape[0]
    n_sg = PEER_SLOTS // slots_per_step
    rows = pl.BlockSpec((1, slots_per_step, tp, PACKED_WIDTH), lambda i, s: (i, s, 0, 0))
    tok = lambda w: pl.BlockSpec((tp, w), lambda i, s: (i, 0))
    return pl.pallas_call(
        functools.partial(_peer_back_body, slots_per_step=slots_per_step),
        grid=(n // tp, n_sg),
        in_specs=[rows, rows, tok(D_MODEL), tok(PEER_SLOTS), tok(D_MODEL), _full((1, D_MODEL))],
        out_specs=tok(D_MODEL),
        out_shape=jax.ShapeDtypeStruct((n, D_MODEL), F32),
        scratch_shapes=[pltpu.VMEM((tp, D_MODEL), F32), pltpu.VMEM((tp, D_MODEL), F32)],
        compiler_params=_cparams(("arbitrary", "arbitrary")), name="peer_back",
    )(ug, vg, xn, gate, h2, g_final.reshape(1, D_MODEL))


PROJ_TILE = 256
ATTN_TILE = 512
PEER_FRONT_TILE = 256
PEER_SLOT_STEP = 32
SAMPLE_PAGES_PER_STEP = 16


def _retrieve_start(x2, attn, cbz, sga, sgb, mk, mv, n_batch, W, xa_tile):
    n = x2.shape[0]
    h1, qx = _merge(x2, attn, cbz, sga, sgb, W["wfo"], W["wco"], W["wo"], W["g_xattn"], W["wxq"], tm=PROJ_TILE)
    o = _cross_attn(qx.reshape(n_batch, n // n_batch, D_MODEL), mk, mv, tm=xa_tile).reshape(n, D_MODEL)
    h2, xn, eidx, etok, gate = _peer_front(h1, o, W["wxo"], W["g_peer"], W["wpq"], W["ka"], W["kb"],
                                           tm=PEER_FRONT_TILE)
    vg, part = _sc_peer(W["pu"], W["pv"], eidx.reshape(-1), etok.reshape(-1), xn)
    return vg, part, gate, h2


def _retrieve_finish(pending, W):
    vg, part, gate, h2 = pending
    shape4 = (gate.shape[0] // PEER_TILE, PEER_SLOTS, PEER_TILE, PACKED_WIDTH)
    return _peer_back2(vg.reshape(shape4), part, gate, h2, W["g_final"], tp=PEER_TILE, slots_per_step=PEER_SLOT_STEP)


def kernel(x_prompt, x_sample, cache_fox_k, cache_fox_v, cache_fox_logf, cache_mem_k, cache_mem_v, state_conv,
           page_table, mem_prompt, g_mix, w_in, b_forget, w_conv, w_fox_out, w_conv_out, w_o, g_xattn, g_mem,
           w_xq, w_xk, w_xv, w_xo, g_peer, w_peer_q, peer_keys_a, peer_keys_b, peer_u, peer_v, g_final):
    depth = g_mix.shape[0]
    assert depth == 1
    l = 0
    b, t, _ = x_prompt.shape
    bd, s, _ = x_sample.shape
    n_pool = cache_fox_k.shape[1]
    mem_len = mem_prompt.shape[1]

    in_w = _prep_in_weights(w_in[l])
    ka, kb = _prep_peer_keys(peer_keys_a[l], peer_keys_b[l])
    W = dict(wfo=w_fox_out[l].astype(BF16), wco=w_conv_out[l].astype(BF16), wo=w_o[l].astype(BF16),
             g_xattn=g_xattn[l], wxq=w_xq[l].astype(BF16), wxo=w_xo[l].astype(BF16), g_peer=g_peer[l],
             wpq=w_peer_q[l].astype(BF16), ka=ka, kb=kb, pu=_pack_table(peer_u[l]), pv=_pack_table(peer_v[l]),
             g_final=g_final)

    mk, mv = _memory_kv(mem_prompt.reshape(b * mem_len, D_MODEL), g_mem[l], w_xk[l].astype(BF16),
                        w_xv[l].astype(BF16), tm=PROJ_TILE)
    mk3 = mk.reshape(b, mem_len, D_MODEL)
    mv3 = mv.reshape(b, mem_len, D_MODEL)
    prev0 = jnp.zeros((SUBLANES, CONV_WIDTH), F32)
    kf_rows, vf_rows, logf_rows, utail_rows, y_rows = [], [], [], [], []
    pending = None
    for r in range(b):
        q, kf, vf, kb16, vb16, logf, cbz, sga, sgb, utail = _in_proj(
            x_prompt[r], g_mix[l], *in_w, b_forget[l], w_conv[l], prev0, n_seq=1, seq_len=t, tm=PROJ_TILE)
        frow = _cumsum_time(logf.T[None])
        attn = _fox_prompt(q[None], kb16[None], vb16[None], frow.transpose(0, 2, 1), frow, tq=ATTN_TILE)[0]
        started = _retrieve_start(x_prompt[r], attn, cbz, sga, sgb, mk3[r:r + 1], mv3[r:r + 1], 1, W, ATTN_TILE)
        if pending is not None:
            y_rows.append(_retrieve_finish(pending, W))
        pending = started
        kf_rows.append(kf)
        vf_rows.append(vf)
        logf_rows.append(logf)
        utail_rows.append(utail)
    heads = lambda a, nb, nt: a.reshape(1, nb, nt, FOX_HEADS, FOX_HEAD_DIM)
    conv_state_p = jnp.stack(utail_rows)[:, SUBLANES - (CONV_K - 1):][None]

    xs = x_sample.reshape(bd * s, D_MODEL)
    prev_s = jnp.concatenate([jnp.zeros((bd, SUBLANES - (CONV_K - 1), CONV_WIDTH), F32), state_conv[l]],
                             axis=1).reshape(bd * SUBLANES, CONV_WIDTH)
    qs, kfs, vfs, kbs, vbs, logfs, cbzs, sgas, sgbs, utails = _in_proj(
        xs, g_mix[l], *in_w, b_forget[l], w_conv[l], prev_s, n_seq=bd, seq_len=s, tm=bd * s)
    assert s * FOX_HEADS == SAMPLE_ROWS
    pad_page = lambda a: jnp.pad(a.reshape(bd, s, FOX_WIDTH).transpose(0, 2, 1),
                                 ((0, 0), (0, 0), (0, PAGE_SIZE - s)))
    lf3 = logfs.reshape(bd, s, FOX_HEADS)
    lfcol = lf3.reshape(bd, SAMPLE_ROWS, 1)
    lfrow = jnp.pad(lf3.transpose(0, 2, 1), ((0, 0), (0, 0), (0, LANES - s)))
    page_view = lambda c: c.transpose(0, 2, 3, 1)
    attn_s = _fox_sample(
        page_table, qs.reshape(bd, s, FOX_WIDTH), pad_page(kbs), pad_page(vbs), lfcol, lfrow,
        page_view(cache_fox_k[l]), page_view(cache_fox_v[l]), cache_fox_logf[l].transpose(0, 2, 1),
        pages_per_step=SAMPLE_PAGES_PER_STEP).reshape(bd * s, FOX_WIDTH)
    cmk = cache_mem_k[l].reshape(bd, mem_len, D_MODEL)
    cmv = cache_mem_v[l].reshape(bd, mem_len, D_MODEL)
    started_s = _retrieve_start(xs, attn_s, cbzs, sgas, sgbs, cmk, cmv, bd, W, s)
    y_rows.append(_retrieve_finish(pending, W))
    y_prompt = jnp.stack(y_rows)
    y_sample = _retrieve_finish(started_s, W).reshape(bd, s, D_MODEL)
    conv_state_s = utails.reshape(bd, SUBLANES, CONV_WIDTH)[:, SUBLANES - (CONV_K - 1):][None]

    return (y_prompt, y_sample,
            heads(jnp.stack(kf_rows), b, t), heads(jnp.stack(vf_rows), b, t), jnp.stack(logf_rows)[None],
            mk.reshape(1, b, mem_len, XA_HEADS, XA_HEAD_DIM), mv.reshape(1, b, mem_len, XA_HEADS, XA_HEAD_DIM),
            conv_state_p,
            heads(kfs, bd, s), heads(vfs, bd, s), logfs.reshape(1, bd, s, FOX_HEADS), conv_state_s)
```

```python
import functools

import jax
import jax.numpy as jnp
from jax import lax
from jax.experimental import pallas as pl
from jax.experimental.pallas import tpu as pltpu
from jax.experimental.pallas import tpu_sc as plsc

F32 = jnp.float32
BF16 = jnp.bfloat16
I32 = jnp.int32

D_MODEL = 1024
FOX_HEADS = 8
FOX_HEAD_DIM = 64
FOX_WIDTH = FOX_HEADS * FOX_HEAD_DIM
CONV_WIDTH = D_MODEL // 2
CONV_K = 3
PAGE_SIZE = 128
XA_HEADS = 4
XA_HEAD_DIM = D_MODEL // XA_HEADS
PEER_HEADS = 8
PEER_N_KEYS = 128
PEER_HALF = 64
PEER_TOPK = 16
PEER_SLOTS = PEER_HEADS * PEER_TOPK
PEER_TILE = 128
RMS_EPS = 1e-6

LANES = 128
SUBLANES = 8
VMEM_LIMIT = 56 * 1024 * 1024
NEG_INF = float("-inf")


def _cparams(sem):
    return pltpu.CompilerParams(dimension_semantics=sem, vmem_limit_bytes=VMEM_LIMIT)


def _full(shape):
    return pl.BlockSpec(shape, lambda *_: (0,) * len(shape))


def _rmsnorm(x, g):
    return x * lax.rsqrt(jnp.mean(x * x, axis=-1, keepdims=True) + RMS_EPS) * g


def _log_sigmoid(x):
    return jnp.minimum(x, 0.0) - jnp.log1p(jnp.exp(-jnp.abs(x)))


def _in_proj_body(x_ref, g_ref, wqkv_ref, wf_ref, wc_ref, wg_ref, bf_ref, wconv_ref, prev_ref,
                  q_ref, kf_ref, vf_ref, kb_ref, vb_ref, logf_ref, cbz_ref, sga_ref, sgb_ref, utail_ref,
                  carry_ref, *, seq_rows):
    tm = x_ref.shape[0]
    xn = _rmsnorm(x_ref[...], g_ref[...]).astype(BF16)

    qkv = jnp.dot(xn, wqkv_ref[...], preferred_element_type=F32)
    k = qkv[:, FOX_WIDTH:2 * FOX_WIDTH]
    v = qkv[:, 2 * FOX_WIDTH:]
    q_ref[...] = (qkv[:, :FOX_WIDTH] * (FOX_HEAD_DIM ** -0.5)).astype(BF16)
    kf_ref[...] = k
    vf_ref[...] = v
    kb_ref[...] = k.astype(BF16)
    vb_ref[...] = v.astype(BF16)

    fl = jnp.dot(xn, wf_ref[...], preferred_element_type=F32)
    logf_ref[...] = _log_sigmoid(fl[:, :FOX_HEADS] + bf_ref[...])

    c3 = jnp.dot(xn, wc_ref[...], preferred_element_type=F32)
    u = c3[:, CONV_WIDTH:2 * CONV_WIDTH] * c3[:, :CONV_WIDTH]
    cb = c3[:, 2 * CONV_WIDTH:]
    rows = lax.broadcasted_iota(I32, (tm, CONV_WIDTH), 0)
    if seq_rows is None:
        @pl.when(pl.program_id(1) == 0)
        def _():
            carry_ref[...] = prev_ref[...]
        hist = carry_ref[...]
        u1 = jnp.where(rows == 0, hist[7:8], pltpu.roll(u, 1, 0))
        u2 = jnp.where(rows == 0, hist[6:7], jnp.where(rows == 1, hist[7:8], pltpu.roll(u, 2, 0)))
        carry_ref[...] = u[tm - SUBLANES:]
        utail_ref[...] = u[tm - SUBLANES:]
    else:
        t = rows % seq_rows
        prev = prev_ref[...]
        u1 = jnp.where(t == 0, pltpu.roll(prev, tm - (seq_rows - 1), 0), pltpu.roll(u, 1, 0))
        u2 = jnp.where(t < 2, pltpu.roll(prev, tm - (seq_rows - 2), 0), pltpu.roll(u, 2, 0))
        utail_ref[...] = u
    wconv = wconv_ref[...]
    conv = u2 * wconv[0:1] + u1 * wconv[1:2] + u * wconv[2:3]
    cbz_ref[...] = (cb * conv).astype(BF16)

    gates = jnp.dot(xn, wg_ref[...], preferred_element_type=F32)
    sga_ref[...] = jax.nn.sigmoid(gates[:, :D_MODEL])
    sgb_ref[...] = jax.nn.sigmoid(gates[:, D_MODEL:])


def _prep_in_weights(w_in):
    o_f = 3 * FOX_WIDTH
    o_c = o_f + FOX_HEADS
    o_g = o_c + 3 * CONV_WIDTH
    wqkv = w_in[:, :o_f].astype(BF16)
    wf = jnp.pad(w_in[:, o_f:o_c], ((0, 0), (0, LANES - FOX_HEADS))).astype(BF16)
    wc = w_in[:, o_c:o_g].astype(BF16)
    wg = w_in[:, o_g:].astype(BF16)
    return wqkv, wf, wc, wg


def _in_proj(x2, g_mix, wqkv, wf, wc, wg, b_forget, w_conv, prev8, *, n_seq, seq_len, tm):
    n = x2.shape[0]
    if seq_len >= tm:
        assert seq_len % tm == 0
        nt = seq_len // tm
        grid = (n_seq, nt)
        tok = lambda w: pl.BlockSpec((tm, w), lambda b, t: (b * nt + t, 0))
        seq8 = pl.BlockSpec((SUBLANES, CONV_WIDTH), lambda b, t: (b, 0))
        seq_rows = None
    else:
        assert seq_len == SUBLANES and tm == n
        grid = (1, 1)
        tok = lambda w: pl.BlockSpec((tm, w), lambda b, t: (0, 0))
        seq8 = pl.BlockSpec((tm, CONV_WIDTH), lambda b, t: (0, 0))
        seq_rows = seq_len
    out_shape = (
        jax.ShapeDtypeStruct((n, FOX_WIDTH), BF16),
        jax.ShapeDtypeStruct((n, FOX_WIDTH), F32),
        jax.ShapeDtypeStruct((n, FOX_WIDTH), F32),
        jax.ShapeDtypeStruct((n, FOX_WIDTH), BF16),
        jax.ShapeDtypeStruct((n, FOX_WIDTH), BF16),
        jax.ShapeDtypeStruct((n, FOX_HEADS), F32),
        jax.ShapeDtypeStruct((n, CONV_WIDTH), BF16),
        jax.ShapeDtypeStruct((n, D_MODEL), F32),
        jax.ShapeDtypeStruct((n, D_MODEL), F32),
        jax.ShapeDtypeStruct((n_seq * SUBLANES, CONV_WIDTH), F32),
    )
    out_specs = (tok(FOX_WIDTH), tok(FOX_WIDTH), tok(FOX_WIDTH), tok(FOX_WIDTH), tok(FOX_WIDTH),
                 tok(FOX_HEADS), tok(CONV_WIDTH), tok(D_MODEL), tok(D_MODEL), seq8)
    in_specs = [tok(D_MODEL), _full((1, D_MODEL)), _full(wqkv.shape), _full(wf.shape), _full(wc.shape),
                _full(wg.shape), _full((1, FOX_HEADS)), _full((CONV_K, CONV_WIDTH)), seq8]
    return pl.pallas_call(
        functools.partial(_in_proj_body, seq_rows=seq_rows),
        grid=grid, in_specs=in_specs, out_specs=out_specs, out_shape=out_shape,
        scratch_shapes=[pltpu.VMEM((SUBLANES, CONV_WIDTH), F32)],
        compiler_params=_cparams(("arbitrary", "arbitrary")),
        name="in_proj",
    )(x2, g_mix.reshape(1, D_MODEL), wqkv, wf, wc, wg, b_forget.reshape(1, FOX_HEADS), w_conv, prev8)


def _lane_prefix_scan(x, lane):
    s = x
    sh = 1
    while sh < LANES:
        s = s + jnp.where(lane >= sh, pltpu.roll(s, sh, 1), 0.0)
        sh *= 2
    return s


def _cumsum_body(x_ref, o_ref):
    n_chunks = x_ref.shape[2] // LANES
    lane = lax.broadcasted_iota(I32, (FOX_HEADS, LANES), 1)

    def step(c, carry):
        off = pl.multiple_of(c * LANES, LANES)
        s = _lane_prefix_scan(x_ref[0, :, pl.ds(off, LANES)], lane) + carry
        o_ref[0, :, pl.ds(off, LANES)] = s
        return s[:, LANES - 1:]

    lax.fori_loop(0, n_chunks, step, jnp.zeros((FOX_HEADS, 1), F32))


def _cumsum_time(logf_t):
    b, h, t = logf_t.shape
    spec = pl.BlockSpec((1, h, t), lambda i: (i, 0, 0))
    return pl.pallas_call(
        _cumsum_body, grid=(b,), in_specs=[spec], out_specs=spec,
        out_shape=jax.ShapeDtypeStruct(logf_t.shape, F32),
        compiler_params=_cparams(("arbitrary",)), name="forget_cumsum",
    )(logf_t)


def _fox_prompt_body(q_ref, k_ref, v_ref, fcol_ref, frow_ref, o_ref, *, tq):
    hp = pl.program_id(1)
    qi = pl.program_id(2)
    q2 = q_ref[0]
    lane = lax.broadcasted_iota(I32, (tq, LANES), 1)
    upper = lane >= FOX_HEAD_DIM
    col8 = lax.broadcasted_iota(I32, (tq, FOX_HEADS), 1)
    fcol8 = fcol_ref[0]
    rowi = lax.broadcasted_iota(I32, (tq, tq), 0)
    coli = lax.broadcasted_iota(I32, (tq, tq), 1)
    zero = jnp.zeros_like(q2)

    outs = []
    for hh in range(2):
        h = 2 * hp + hh
        qh = jnp.where(upper, q2, zero) if hh else jnp.where(upper, zero, q2)
        fq = jnp.sum(jnp.where(col8 == h, fcol8, 0.0), axis=1, keepdims=True)

        def block(kb, carry, masked):
            m, l, acc = carry
            off = pl.multiple_of(kb * tq, tq)
            kk = k_ref[0, pl.ds(off, tq), :]
            vv = v_ref[0, pl.ds(off, tq), :]
            fk = frow_ref[0, pl.ds(h, 1), pl.ds(off, tq)]
            s = lax.dot_general(qh, kk, (((1,), (1,)), ((), ())), preferred_element_type=F32)
            s = s + fq - fk
            if masked:
                s = jnp.where(coli <= rowi, s, NEG_INF)
            m_new = jnp.maximum(m, jnp.max(s, axis=1, keepdims=True))
            p = jnp.exp(s - m_new)
            alpha = jnp.exp(m - m_new)
            l = alpha * l + jnp.sum(p, axis=1, keepdims=True)
            acc = alpha * acc + jnp.dot(p.astype(BF16), vv, preferred_element_type=F32)
            return m_new, l, acc

        init = (jnp.full((tq, 1), NEG_INF, F32), jnp.zeros((tq, 1), F32), jnp.zeros((tq, LANES), F32))
        carry = lax.fori_loop(0, qi, functools.partial(block, masked=False), init)
        m, l, acc = block(qi, carry, True)
        outs.append(acc / l)
    o_ref[0] = jnp.where(upper, outs[1], outs[0]).astype(o_ref.dtype)


def _fox_prompt(q, k, v, fcol, frow, *, tq):
    b, t, _ = q.shape
    n_pairs = FOX_WIDTH // LANES
    return pl.pallas_call(
        functools.partial(_fox_prompt_body, tq=tq),
        grid=(b, n_pairs, t // tq),
        in_specs=[
            pl.BlockSpec((1, tq, LANES), lambda i, p, j: (i, j, p)),
            pl.BlockSpec((1, t, LANES), lambda i, p, j: (i, 0, p)),
            pl.BlockSpec((1, t, LANES), lambda i, p, j: (i, 0, p)),
            pl.BlockSpec((1, tq, FOX_HEADS), lambda i, p, j: (i, j, 0)),
            pl.BlockSpec((1, FOX_HEADS, t), lambda i, p, j: (i, 0, 0)),
        ],
        out_specs=pl.BlockSpec((1, tq, LANES), lambda i, p, j: (i, j, p)),
        out_shape=jax.ShapeDtypeStruct((b, t, FOX_WIDTH), BF16),
        compiler_params=_cparams(("arbitrary", "arbitrary", "arbitrary")),
        name="fox_prompt",
    )(q, k, v, fcol, frow)


SAMPLE_ROWS = 64


def _lane_suffix_excl(x, lane):
    s = jnp.where(lane < LANES - 1, pltpu.roll(x, LANES - 1, 1), 0.0)
    sh = 1
    while sh < LANES:
        s = s + jnp.where(lane < LANES - sh, pltpu.roll(s, LANES - sh, 1), 0.0)
        sh *= 2
    return s


def _fox_sample_body(pt_ref, q_ref, kn_ref, vn_ref, lfcol_ref, lfrow_ref, *rest, pages_per_step):
    pg = pages_per_step
    k_refs = rest[:pg]
    v_refs = rest[pg:2 * pg]
    lf_refs = rest[2 * pg:3 * pg]
    o_ref = rest[3 * pg]
    m_ref, l_ref, acc_ref, tot_ref, qbd_ref, rt_ref = rest[3 * pg + 1:]
    g = pl.program_id(1)
    lane8 = lax.broadcasted_iota(I32, (FOX_HEADS, LANES), 1)
    row = lax.broadcasted_iota(I32, (SAMPLE_ROWS, FOX_WIDTH), 0)
    lane = lax.broadcasted_iota(I32, (SAMPLE_ROWS, FOX_WIDTH), 1)
    head_mask = (lane // FOX_HEAD_DIM) == (row % FOX_HEADS)
    nt = (((1,), (1,)), ((), ()))

    def tile8(x):
        return jnp.concatenate([x] * (SAMPLE_ROWS // FOX_HEADS), axis=0)

    @pl.when(g == 0)
    def _():
        q = q_ref[0]
        qrows = jnp.concatenate(
            [jnp.broadcast_to(q[t:t + 1, :], (FOX_HEADS, FOX_WIDTH)) for t in range(q.shape[0])], axis=0)
        qbd = jnp.where(head_mask, qrows, jnp.zeros_like(qrows))
        qbd_ref[...] = qbd
        r1 = lax.broadcasted_iota(I32, (SAMPLE_ROWS, 1), 0)
        x = lfcol_ref[0]
        s = jnp.where(r1 < SAMPLE_ROWS - 8, pltpu.roll(x, SAMPLE_ROWS - 8, 0), 0.0)
        for sh in (8, 16, 32):
            s = s + jnp.where(r1 < SAMPLE_ROWS - sh, pltpu.roll(s, SAMPLE_ROWS - sh, 0), 0.0)
        rt = -s
        rt_ref[...] = rt
        lfn = lfrow_ref[0]
        excl = _lane_suffix_excl(lfn, lane8)
        tot_ref[...] = excl[:, 0:1] + lfn[:, 0:1]
        sc = jnp.dot(qbd, kn_ref[0], preferred_element_type=F32) + rt + tile8(excl)
        rr = lax.broadcasted_iota(I32, (SAMPLE_ROWS, LANES), 0)
        cc = lax.broadcasted_iota(I32, (SAMPLE_ROWS, LANES), 1)
        sc = jnp.where(cc <= rr // FOX_HEADS, sc, NEG_INF)
        m_new = jnp.max(sc, axis=1, keepdims=True)
        p = jnp.exp(sc - m_new)
        m_ref[...] = m_new
        l_ref[...] = jnp.sum(p, axis=1, keepdims=True)
        acc_ref[...] = lax.dot_general(p.astype(BF16), vn_ref[0], nt, preferred_element_type=F32)

    qbd = qbd_ref[...]
    tot = tot_ref[...]
    biases, ks, vs = [], [], []
    for i in reversed(range(pg)):
        lfp = lf_refs[i][0]
        r = _lane_suffix_excl(lfp, lane8) + tot
        tot = r[:, 0:1] + lfp[:, 0:1]
        biases.append(tile8(r))
        ks.append(k_refs[i][0].reshape(FOX_WIDTH, PAGE_SIZE).astype(BF16))
        vs.append(v_refs[i][0].reshape(FOX_WIDTH, PAGE_SIZE).astype(BF16))
    tot_ref[...] = tot
    sc = jnp.dot(qbd, jnp.concatenate(ks, axis=1), preferred_element_type=F32)
    sc = sc + rt_ref[...] + jnp.concatenate(biases, axis=1)
    m_old = m_ref[...]
    m_new = jnp.maximum(m_old, jnp.max(sc, axis=1, keepdims=True))
    p = jnp.exp(sc - m_new)
    alpha = jnp.exp(m_old - m_new)
    l_ref[...] = alpha * l_ref[...] + jnp.sum(p, axis=1, keepdims=True)
    acc_ref[...] = alpha * acc_ref[...] + lax.dot_general(
        p.astype(BF16), jnp.concatenate(vs, axis=1), nt, preferred_element_type=F32)
    m_ref[...] = m_new

    @pl.when(g == pl.num_programs(1) - 1)
    def _():
        full = jnp.where(head_mask, acc_ref[...] / l_ref[...], 0.0)
        o_ref[0] = jnp.concatenate(
            [jnp.sum(full[FOX_HEADS * t:FOX_HEADS * (t + 1)], axis=0, keepdims=True)
             for t in range(SAMPLE_ROWS // FOX_HEADS)], axis=0).astype(o_ref.dtype)


def _fox_sample(page_table, q, kn_t, vn_t, lfcol, lfrow_pad, cache_kt, cache_vt, cache_lf_t, *, pages_per_step):
    bd, n_pages = page_table.shape
    pg = pages_per_step
    assert n_pages % pg == 0
    ng = n_pages // pg
    s_new = q.shape[1]

    def page_spec(shape, i):
        zeros = (0,) * (len(shape) - 1)
        return pl.BlockSpec(shape, lambda b, g, pt: (pt[b, (ng - 1 - g) * pg + i],) + zeros)

    req = lambda shape: pl.BlockSpec(shape, lambda b, g, pt: (b, 0, 0))
    in_specs = [req((1, s_new, FOX_WIDTH)), req((1, FOX_WIDTH, PAGE_SIZE)), req((1, FOX_WIDTH, PAGE_SIZE)),
                req((1, SAMPLE_ROWS, 1)), req((1, FOX_HEADS, LANES))]
    in_specs += [page_spec((1, FOX_HEADS, FOX_HEAD_DIM, PAGE_SIZE), i) for i in range(pg)]
    in_specs += [page_spec((1, FOX_HEADS, FOX_HEAD_DIM, PAGE_SIZE), i) for i in range(pg)]
    in_specs += [page_spec((1, FOX_HEADS, PAGE_SIZE), i) for i in range(pg)]
    grid_spec = pltpu.PrefetchScalarGridSpec(
        num_scalar_prefetch=1, grid=(bd, ng), in_specs=in_specs,
        out_specs=req((1, s_new, FOX_WIDTH)),
        scratch_shapes=[pltpu.VMEM((SAMPLE_ROWS, 1), F32), pltpu.VMEM((SAMPLE_ROWS, 1), F32),
                        pltpu.VMEM((SAMPLE_ROWS, FOX_WIDTH), F32), pltpu.VMEM((FOX_HEADS, 1), F32),
                        pltpu.VMEM((SAMPLE_ROWS, FOX_WIDTH), BF16), pltpu.VMEM((SAMPLE_ROWS, 1), F32)])
    return pl.pallas_call(
        functools.partial(_fox_sample_body, pages_per_step=pg),
        grid_spec=grid_spec, out_shape=jax.ShapeDtypeStruct((bd, s_new, FOX_WIDTH), BF16),
        compiler_params=_cparams(("arbitrary", "arbitrary")), name="fox_sample",
    )(page_table, q, kn_t, vn_t, lfcol, lfrow_pad, *([cache_kt] * pg), *([cache_vt] * pg), *([cache_lf_t] * pg))


def _merge_body(x_ref, attn_ref, cbz_ref, sga_ref, sgb_ref, wfo_ref, wco_ref, wo_ref, gx_ref, wxq_ref,
                h1_ref, qx_ref):
    ya = jnp.dot(attn_ref[...], wfo_ref[...], preferred_element_type=F32)
    yb = jnp.dot(cbz_ref[...], wco_ref[...], preferred_element_type=F32)
    mix = (sga_ref[...] * ya + sgb_ref[...] * yb).astype(BF16)
    h1 = x_ref[...] + jnp.dot(mix, wo_ref[...], preferred_element_type=F32)
    h1_ref[...] = h1
    xn = _rmsnorm(h1, gx_ref[...]).astype(BF16)
    qx_ref[...] = (jnp.dot(xn, wxq_ref[...], preferred_element_type=F32) * (XA_HEAD_DIM ** -0.5)).astype(BF16)


def _merge(x2, attn, cbz, sga, sgb, wfo, wco, wo, g_xattn, wxq, *, tm):
    n = x2.shape[0]
    tok = lambda w: pl.BlockSpec((tm, w), lambda i: (i, 0))
    return pl.pallas_call(
        _merge_body, grid=(n // tm,),
        in_specs=[tok(D_MODEL), tok(FOX_WIDTH), tok(CONV_WIDTH), tok(D_MODEL), tok(D_MODEL),
                  _full(wfo.shape), _full(wco.shape), _full(wo.shape), _full((1, D_MODEL)), _full(wxq.shape)],
        out_specs=(tok(D_MODEL), tok(D_MODEL)),
        out_shape=(jax.ShapeDtypeStruct((n, D_MODEL), F32), jax.ShapeDtypeStruct((n, D_MODEL), BF16)),
        compiler_params=_cparams(("arbitrary",)), name="merge",
    )(x2, attn, cbz, sga, sgb, wfo, wco, wo, g_xattn.reshape(1, D_MODEL), wxq)


def _memory_kv_body(m_ref, g_ref, wk_ref, wv_ref, mk_ref, mv_ref):
    mn = _rmsnorm(m_ref[...], g_ref[...]).astype(BF16)
    mk_ref[...] = jnp.dot(mn, wk_ref[...], preferred_element_type=F32)
    mv_ref[...] = jnp.dot(mn, wv_ref[...], preferred_element_type=F32)


def _memory_kv(mem2, g_mem, wxk, wxv, *, tm):
    n = mem2.shape[0]
    tok = pl.BlockSpec((tm, D_MODEL), lambda i: (i, 0))
    return pl.pallas_call(
        _memory_kv_body, grid=(n // tm,),
        in_specs=[tok, _full((1, D_MODEL)), _full(wxk.shape), _full(wxv.shape)],
        out_specs=(tok, tok),
        out_shape=(jax.ShapeDtypeStruct((n, D_MODEL), F32),) * 2,
        compiler_params=_cparams(("arbitrary",)), name="memory_kv",
    )(mem2, g_mem.reshape(1, D_MODEL), wxk, wxv)


def _cross_attn_body(q_ref, mk_ref, mv_ref, o_ref):
    q = q_ref[0]
    outs = []
    for h in range(XA_HEADS):
        sl = slice(h * XA_HEAD_DIM, (h + 1) * XA_HEAD_DIM)
        kh = mk_ref[0, :, sl].astype(BF16)
        vh = mv_ref[0, :, sl].astype(BF16)
        s = lax.dot_general(q[:, sl], kh, (((1,), (1,)), ((), ())), preferred_element_type=F32)
        p = jnp.exp(s - jnp.max(s, axis=1, keepdims=True))
        p = p / jnp.sum(p, axis=1, keepdims=True)
        outs.append(jnp.dot(p.astype(BF16), vh, preferred_element_type=F32))
    o_ref[0] = jnp.concatenate(outs, axis=1).astype(o_ref.dtype)


def _cross_attn(qx, mk, mv, *, tm):
    b, t, _ = qx.shape
    m = mk.shape[1]
    tok = pl.BlockSpec((1, tm, D_MODEL), lambda i, j: (i, j, 0))
    mem = pl.BlockSpec((1, m, D_MODEL), lambda i, j: (i, 0, 0))
    return pl.pallas_call(
        _cross_attn_body, grid=(b, t // tm), in_specs=[tok, mem, mem], out_specs=tok,
        out_shape=jax.ShapeDtypeStruct((b, t, D_MODEL), BF16),
        compiler_params=_cparams(("arbitrary", "arbitrary")), name="cross_attn",
    )(qx, mk, mv)


PEER_PAIRS = tuple((i, j) for i in range(PEER_TOPK) for j in range(PEER_TOPK) if (i + 1) * (j + 1) <= PEER_TOPK)
PEER_CAND_ROWS = -(-len(PEER_PAIRS) // SUBLANES) * SUBLANES


def _top_rows(s, k, payload=None):
    n = s.shape[0]
    iota = lax.broadcasted_iota(I32, s.shape, 0)
    vals, ids = [], []
    for _ in range(k):
        m = jnp.max(s, axis=0, keepdims=True)
        r = jnp.min(jnp.where(s == m, iota, n), axis=0, keepdims=True)
        hit = iota == r
        vals.append(m)
        ids.append(r if payload is None else jnp.max(jnp.where(hit, payload, -1), axis=0, keepdims=True))
        s = jnp.where(hit, NEG_INF, s)
    return vals, ids


def _peer_front_body(h1_ref, o_ref, wxo_ref, gp_ref, wpq_ref, ka_ref, kb_ref,
                     h2_ref, xn_ref, eidx_ref, etok_ref, gate_ref, qp_ref, e_ref, g_ref):
    tm = h1_ref.shape[0]
    h2 = h1_ref[...] + jnp.dot(o_ref[...], wxo_ref[...], preferred_element_type=F32)
    h2_ref[...] = h2
    xn = _rmsnorm(h2, gp_ref[...]).astype(BF16)
    xn_ref[...] = xn.astype(F32)
    qp_ref[...] = jnp.dot(xn, wpq_ref[...], preferred_element_type=F32).astype(BF16)

    def head(h, _):
        qh = qp_ref[:, pl.ds(pl.multiple_of(h * LANES, LANES), LANES)]
        nt = (((1,), (1,)), ((), ()))
        sa = lax.dot_general(ka_ref[h], qh, nt, preferred_element_type=F32)
        sb = lax.dot_general(kb_ref[h], qh, nt, preferred_element_type=F32)
        va, ia = _top_rows(sa, PEER_TOPK)
        vb, ib = _top_rows(sb, PEER_TOPK)
        pad = PEER_CAND_ROWS - len(PEER_PAIRS)
        cand = jnp.concatenate([va[i] + vb[j] for i, j in PEER_PAIRS]
                               + [jnp.full((pad, tm), NEG_INF, F32)], axis=0)
        ceid = jnp.concatenate([ia[i] * PEER_N_KEYS + ib[j] for i, j in PEER_PAIRS]
                               + [jnp.zeros((pad, tm), I32)], axis=0)
        ts, te = _top_rows(cand, PEER_TOPK, payload=ceid)
        ex = [jnp.exp(t - ts[0]) for t in ts]
        den = ex[0]
        for e in ex[1:]:
            den = den + e
        row0 = pl.multiple_of(h * PEER_TOPK, PEER_TOPK)
        e_ref[pl.ds(row0, PEER_TOPK), :] = jnp.concatenate(te, axis=0)
        g_ref[pl.ds(row0, PEER_TOPK), :] = jnp.concatenate([e / den for e in ex], axis=0)
        return 0

    lax.fori_loop(0, PEER_HEADS, head, 0)
    for c in range(eidx_ref.shape[0]):
        eidx_ref[c] = e_ref[:, c * PEER_TILE:(c + 1) * PEER_TILE]
    etok_ref[...] = e_ref[...].T
    gate_ref[...] = g_ref[...].T


def _prep_peer_keys(keys_a, keys_b):
    z = jnp.zeros_like(keys_a)
    return (jnp.concatenate([keys_a, z], axis=-1).astype(BF16), jnp.concatenate([z, keys_b], axis=-1).astype(BF16))


def _peer_front(h1, o, wxo, g_peer, wpq, ka_pad, kb_pad, *, tm):
    n = h1.shape[0]
    assert tm % PEER_TILE == 0
    sub = tm // PEER_TILE
    tok = lambda w: pl.BlockSpec((tm, w), lambda i: (i, 0))
    return pl.pallas_call(
        _peer_front_body, grid=(n // tm,),
        in_specs=[tok(D_MODEL), tok(D_MODEL), _full(wxo.shape), _full((1, D_MODEL)), _full(wpq.shape),
                  _full(ka_pad.shape), _full(kb_pad.shape)],
        out_specs=(tok(D_MODEL), tok(D_MODEL), pl.BlockSpec((sub, PEER_SLOTS, PEER_TILE), lambda i: (i, 0, 0)),
                   tok(PEER_SLOTS), tok(PEER_SLOTS)),
        out_shape=(jax.ShapeDtypeStruct((n, D_MODEL), F32), jax.ShapeDtypeStruct((n, D_MODEL), F32),
                   jax.ShapeDtypeStruct((n // PEER_TILE, PEER_SLOTS, PEER_TILE), I32),
                   jax.ShapeDtypeStruct((n, PEER_SLOTS), I32),
                   jax.ShapeDtypeStruct((n, PEER_SLOTS), F32)),
        scratch_shapes=[pltpu.VMEM((tm, PEER_HEADS * LANES), BF16), pltpu.VMEM((PEER_SLOTS, tm), I32),
                        pltpu.VMEM((PEER_SLOTS, tm), F32)],
        compiler_params=_cparams(("arbitrary",)), name="peer_front",
    )(h1, o, wxo, g_peer.reshape(1, D_MODEL), wpq, ka_pad, kb_pad)


PACKED_WIDTH = D_MODEL // 2


def _pack_table(t):
    b = lax.bitcast_convert_type(t.astype(BF16), jnp.uint16).astype(jnp.uint32)
    return lax.bitcast_convert_type(b[:, :PACKED_WIDTH] | (b[:, PACKED_WIDTH:] << 16), I32)


def _unpack(w):
    lo = lax.bitcast_convert_type(w << 16, F32)
    hi = lax.bitcast_convert_type(w & jnp.int32(-65536), F32)
    return lo, hi


SC_CORES = 2
SC_SUBCORES = 16
SC_WINDOW = 64


def _sc_gather2(table_u, table_v, idx):
    r = idx.shape[0]
    workers = SC_CORES * SC_SUBCORES
    assert r % (workers * SC_WINDOW) == 0
    per_worker = r // workers
    n_win = per_worker // SC_WINDOW
    width = table_u.shape[1]
    mesh = plsc.VectorSubcoreMesh(core_axis_name="c", subcore_axis_name="s")
    out = jax.ShapeDtypeStruct((r, width), table_u.dtype)

    @functools.partial(
        pl.kernel, mesh=mesh, out_type=(out, out),
        scratch_types=[pltpu.VMEM((SC_WINDOW,), I32), pltpu.VMEM((SC_WINDOW, width), table_u.dtype),
                       pltpu.VMEM((SC_WINDOW, width), table_u.dtype),
                       pltpu.SemaphoreType.DMA, pltpu.SemaphoreType.DMA])
    def gather(u_hbm, v_hbm, idx_hbm, ou_hbm, ov_hbm, idx_v, ru_v, rv_v, sem_u, sem_v):
        wid = lax.axis_index("s") * SC_CORES + lax.axis_index("c")
        base = wid * per_worker

        @pl.loop(0, n_win)
        def _(i):
            off = base + i * SC_WINDOW
            pltpu.sync_copy(idx_hbm.at[pl.ds(off, SC_WINDOW)], idx_v)
            cu = pltpu.async_copy(u_hbm.at[idx_v], ru_v, sem_u)
            cv = pltpu.async_copy(v_hbm.at[idx_v], rv_v, sem_v)
            cu.wait()
            pltpu.sync_copy(ru_v, ou_hbm.at[pl.ds(off, SC_WINDOW)])
            cv.wait()
            pltpu.sync_copy(rv_v, ov_hbm.at[pl.ds(off, SC_WINDOW)])

    return gather(table_u, table_v, idx)


def _peer_back_body(ug_ref, vg_ref, xn_ref, gate_ref, h2_ref, gf_ref, y_ref, x32_ref, out_ref, *, slots_per_step):
    sg = pl.program_id(1)
    tp = xn_ref.shape[0]
    n_groups = tp // SUBLANES
    lane = lax.broadcasted_iota(I32, (SUBLANES, PEER_SLOTS), 1)
    slot0 = sg * slots_per_step

    @pl.when(sg == 0)
    def _():
        x32_ref[...] = xn_ref[...].astype(F32)
        out_ref[...] = jnp.zeros_like(out_ref)

    def rows_of(r):
        return pl.ds(pl.multiple_of(r * SUBLANES, SUBLANES), SUBLANES)

    def u_pass(r):
        rows = rows_of(r)
        xlo = x32_ref[rows, :PACKED_WIDTH]
        xhi = x32_ref[rows, PACKED_WIDTH:]
        act = jnp.zeros((SUBLANES, PEER_SLOTS), F32)
        for j in range(slots_per_step):
            ulo, uhi = _unpack(ug_ref[0, j, rows, :])
            a = jnp.sum(ulo * xlo + uhi * xhi, axis=1, keepdims=True)
            act = jnp.where(lane == slot0 + j, a, act)
        return gate_ref[rows, :] * jax.nn.gelu(act)

    def v_pass(r, wts):
        rows = rows_of(r)
        olo = out_ref[rows, :PACKED_WIDTH]
        ohi = out_ref[rows, PACKED_WIDTH:]
        for j in range(slots_per_step):
            w = jnp.sum(jnp.where(lane == slot0 + j, wts, 0.0), axis=1, keepdims=True)
            vlo, vhi = _unpack(vg_ref[0, j, rows, :])
            olo = olo + w * vlo
            ohi = ohi + w * vhi
        out_ref[rows, :PACKED_WIDTH] = olo
        out_ref[rows, PACKED_WIDTH:] = ohi

    def group(r, wts):
        nxt = u_pass(r)
        v_pass(r - 1, wts)
        return nxt

    last = lax.fori_loop(1, n_groups, group, u_pass(0))
    v_pass(n_groups - 1, last)

    @pl.when(sg == pl.num_programs(1) - 1)
    def _():
        y_ref[...] = _rmsnorm(h2_ref[...] + out_ref[...], gf_ref[...])


SC_ROWS = 16
SC_U_BUFS = 4
SC_LANES = 16
PART_WIDTH = PEER_SLOTS * SC_LANES


def _sc_peer(table_u, table_v, idx_v, idx_u, x32):
    n = x32.shape[0]
    workers = SC_CORES * SC_SUBCORES
    tw = n // workers
    assert n % (2 * workers) == 0
    wins = PEER_SLOTS // SC_ROWS
    assert wins % SC_U_BUFS == 0
    words = SC_ROWS * SC_LANES
    n_col = PACKED_WIDTH // SC_LANES
    mesh = plsc.VectorSubcoreMesh(core_axis_name="c", subcore_axis_name="s")
    dma = pltpu.SemaphoreType.DMA

    @functools.partial(
        pl.kernel, mesh=mesh, compiler_params=pltpu.CompilerParams(needs_layout_passes=False),
        out_type=(jax.ShapeDtypeStruct((n * PEER_SLOTS, PACKED_WIDTH), I32),
                  jax.ShapeDtypeStruct((n, PART_WIDTH), F32)),
        scratch_types=[pltpu.VMEM((tw * PEER_SLOTS,), I32), pltpu.VMEM((tw * PEER_SLOTS,), I32),
                       pltpu.VMEM((SC_U_BUFS, SC_ROWS), I32), pltpu.VMEM((2, SC_ROWS), I32),
                       pltpu.VMEM((SC_U_BUFS, SC_ROWS, PACKED_WIDTH), I32),
                       pltpu.VMEM((2, SC_ROWS, PACKED_WIDTH), I32),
                       pltpu.VMEM((2, D_MODEL), F32), pltpu.VMEM((2, PART_WIDTH), F32)] + [dma] * (SC_U_BUFS + 8))
    def body(u_hbm, v_hbm, idxv_hbm, idxu_hbm, x_hbm, ov_hbm, part_hbm,
             idxu_all, idxv_all, iu, iv, ru, rv, xv, pv, *sems):
        su = sems[:SC_U_BUFS]
        sv, sw, sx, sp = (sems[SC_U_BUFS + 2 * q:SC_U_BUFS + 2 * q + 2] for q in range(4))
        ahead = SC_U_BUFS - 1
        wid = lax.axis_index("s") * SC_CORES + lax.axis_index("c")
        tok0 = wid * tw
        row0 = tok0 * PEER_SLOTS
        pltpu.sync_copy(idxu_hbm.at[pl.ds(row0, tw * PEER_SLOTS)], idxu_all)
        pltpu.sync_copy(idxv_hbm.at[pl.ds(row0, tw * PEER_SLOTS)], idxv_all)

        def u_gather(b):
            return pltpu.make_async_copy(u_hbm.at[iu.at[b]], ru.at[b], su[b])

        def v_gather(b):
            return pltpu.make_async_copy(v_hbm.at[iv.at[b]], rv.at[b], sv[b])

        def v_write(b, k):
            return pltpu.make_async_copy(rv.at[b], ov_hbm.at[pl.ds(row0 + k * SC_ROWS, SC_ROWS)], sw[b])

        def x_load(par, i):
            return pltpu.make_async_copy(x_hbm.at[tok0 + i], xv.at[par], sx[par])

        def part_write(par, i):
            return pltpu.make_async_copy(pv.at[par], part_hbm.at[tok0 + i], sp[par])

        def start_u(b, k):
            iu[b, :] = idxu_all[pl.ds(pl.multiple_of(k * SC_ROWS, SC_ROWS), SC_ROWS)]
            u_gather(b).start()

        def start_v(b, k):
            iv[b, :] = idxv_all[pl.ds(pl.multiple_of(k * SC_ROWS, SC_ROWS), SC_ROWS)]
            v_gather(b).start()

        def dots(b, par, w):
            for s in range(SC_ROWS):
                pv[par, pl.ds(w * words + s * SC_LANES, SC_LANES)] = jnp.zeros((SC_LANES,), F32)

            @pl.loop(0, n_col // 2)
            def _(j):
                vals = []
                for h in range(2):
                    c = pl.multiple_of((2 * j + h) * SC_LANES, SC_LANES)
                    xlo = xv[par, pl.ds(c, SC_LANES)]
                    xhi = xv[par, pl.ds(PACKED_WIDTH + c, SC_LANES)]
                    rows = [ru[b, s, pl.ds(c, SC_LANES)] for s in range(SC_ROWS)]
                    vals.append([plsc.bitcast(r << 16, F32) * xlo + plsc.bitcast(r & jnp.int32(-65536), F32) * xhi
                                 for r in rows])
                for s in range(SC_ROWS):
                    plsc.addupdate(pv.at[par, pl.ds(w * words + s * SC_LANES, SC_LANES)], vals[0][s] + vals[1][s])

        x_load(0, 0).start()
        for k0 in range(ahead):
            start_u(k0, k0)
        start_v(0, 0)

        @pl.loop(0, tw // 2)
        def _(i2):
            for par in range(2):
                i = 2 * i2 + par
                more_tokens = i2 < tw // 2 - 1 if par else None
                if par == 0:
                    x_load(1, i + 1).start()
                else:
                    @pl.when(more_tokens)
                    def _():
                        x_load(0, i + 1).start()
                x_load(par, i).wait()

                @pl.when(i2 >= 1)
                def _():
                    part_write(par, i).wait()

                for w in range(wins):
                    b, nb = w % 2, 1 - w % 2
                    bu = w % SC_U_BUFS
                    k = i * wins + w
                    if w >= 1 or par == 1:
                        v_write(nb, k).wait()
                    else:
                        @pl.when(i2 >= 1)
                        def _():
                            v_write(nb, k).wait()
                    if w < wins - 1 or par == 0:
                        start_v(nb, k + 1)
                    else:
                        @pl.when(more_tokens)
                        def _():
                            start_v(nb, k + 1)
                    if w < wins - ahead or par == 0:
                        start_u((w + ahead) % SC_U_BUFS, k + ahead)
                    else:
                        @pl.when(more_tokens)
                        def _():
                            start_u((w + ahead) % SC_U_BUFS, k + ahead)
                    u_gather(bu).wait()
                    dots(bu, par, w)
                    v_gather(b).wait()
                    v_write(b, k).start()
                part_write(par, i).start()

        v_write(1, 0).wait()
        part_write(0, 0).wait()
        part_write(1, 0).wait()

    return body(table_u, table_v, idx_v, idx_u, x32)


def _peer_back2_body(vg_ref, part_ref, ones_ref, gate_ref, h2_ref, gf_ref, y_ref, wts_ref, out_ref, *, slots_per_step):
    sg = pl.program_id(1)
    tp = gate_ref.shape[0]
    lane = lax.broadcasted_iota(I32, (SUBLANES, PEER_SLOTS), 1)
    slot0 = sg * slots_per_step

    @pl.when(sg == 0)
    def _():
        act = jnp.dot(part_ref[...], ones_ref[...], precision=lax.Precision.HIGHEST, preferred_element_type=F32)
        wts_ref[...] = gate_ref[...] * jax.nn.gelu(act)
        out_ref[...] = jnp.zeros_like(out_ref)

    def group(r, _):
        rows = pl.ds(pl.multiple_of(r * SUBLANES, SUBLANES), SUBLANES)
        wts = wts_ref[rows, :]
        olo = out_ref[rows, :PACKED_WIDTH]
        ohi = out_ref[rows, PACKED_WIDTH:]
        for j in range(slots_per_step):
            w = jnp.sum(jnp.where(lane == slot0 + j, wts, 0.0), axis=1, keepdims=True)
            vlo, vhi = _unpack(vg_ref[0, j, rows, :])
            olo = olo + w * vlo
            ohi = ohi + w * vhi
        out_ref[rows, :PACKED_WIDTH] = olo
        out_ref[rows, PACKED_WIDTH:] = ohi
        return 0

    lax.fori_loop(0, tp // SUBLANES, group, 0)

    @pl.when(sg == pl.num_programs(1) - 1)
    def _():
        y_ref[...] = _rmsnorm(h2_ref[...] + out_ref[...], gf_ref[...])


def _peer_back2(vg, part, gate, h2, g_final, *, tp, slots_per_step):
    n = gate.shape[0]
    n_sg = PEER_SLOTS // slots_per_step
    ones = (jnp.arange(PART_WIDTH)[:, None] // SC_LANES == jnp.arange(PEER_SLOTS)[None, :]).astype(F32)
    rows = pl.BlockSpec((1, slots_per_step, tp, PACKED_WIDTH), lambda i, s: (i, s, 0, 0))
    tok = lambda w: pl.BlockSpec((tp, w), lambda i, s: (i, 0))
    return pl.pallas_call(
        functools.partial(_peer_back2_body, slots_per_step=slots_per_step),
        grid=(n // tp, n_sg),
        in_specs=[rows, tok(PART_WIDTH), _full(ones.shape), tok(PEER_SLOTS), tok(D_MODEL), _full((1, D_MODEL))],
        out_specs=tok(D_MODEL),
        out_shape=jax.ShapeDtypeStruct((n, D_MODEL), F32),
        scratch_shapes=[pltpu.VMEM((tp, PEER_SLOTS), F32), pltpu.VMEM((tp, D_MODEL), F32)],
        compiler_params=_cparams(("arbitrary", "arbitrary")), name="peer_back",
    )(vg, part, ones, gate, h2, g_final.reshape(1, D_MODEL))


def _peer_back(ug, vg, xn, gate, h2, g_final, *, tp, slots_per_step):
    n = xn.shape[0]
    n_sg = PEER_SLOTS // slots_per_step
    rows = pl.BlockSpec((1, slots_per_step, tp, PACKED_WIDTH), lambda i, s: (i, s, 0, 0))
    tok = lambda w: pl.BlockSpec((tp, w), lambda i, s: (i, 0))
    return pl.pallas_call(
        functools.partial(_peer_back_body, slots_per_step=slots_per_step),
        grid=(n // tp, n_sg),
        in_specs=[rows, rows, tok(D_MODEL), tok(PEER_SLOTS), tok(D_MODEL), _full((1, D_MODEL))],
        out_specs=tok(D_MODEL),
        out_shape=jax.ShapeDtypeStruct((n, D_MODEL), F32),
        scratch_shapes=[pltpu.VMEM((tp, D_MODEL), F32), pltpu.VMEM((tp, D_MODEL), F32)],
        compiler_params=_cparams(("arbitrary", "arbitrary")), name="peer_back",
    )(ug, vg, xn, gate, h2, g_final.reshape(1, D_MODEL))


PROJ_TILE = 256
ATTN_TILE = 512
PEER_FRONT_TILE = 256
PEER_SLOT_STEP = 32
SAMPLE_PAGES_PER_STEP = 16


def _retrieve_start(x2, attn, cbz, sga, sgb, mk, mv, n_batch, W, xa_tile):
    n = x2.shape[0]
    h1, qx = _merge(x2, attn, cbz, sga, sgb, W["wfo"], W["wco"], W["wo"], W["g_xattn"], W["wxq"], tm=PROJ_TILE)
    o = _cross_attn(qx.reshape(n_batch, n // n_batch, D_MODEL), mk, mv, tm=xa_tile).reshape(n, D_MODEL)
    h2, xn, eidx, etok, gate = _peer_front(h1, o, W["wxo"], W["g_peer"], W["wpq"], W["ka"], W["kb"],
                                           tm=PEER_FRONT_TILE)
    vg, part = _sc_peer(W["pu"], W["pv"], eidx.reshape(-1), etok.reshape(-1), xn)
    return vg, part, gate, h2


def _retrieve_finish(pending, W):
    vg, part, gate, h2 = pending
    shape4 = (gate.shape[0] // PEER_TILE, PEER_SLOTS, PEER_TILE, PACKED_WIDTH)
    return _peer_back2(vg.reshape(shape4), part, gate, h2, W["g_final"], tp=PEER_TILE, slots_per_step=PEER_SLOT_STEP)


def kernel(x_prompt, x_sample, cache_fox_k, cache_fox_v, cache_fox_logf, cache_mem_k, cache_mem_v, state_conv,
           page_table, mem_prompt, g_mix, w_in, b_forget, w_conv, w_fox_out, w_conv_out, w_o, g_xattn, g_mem,
           w_xq, w_xk, w_xv, w_xo, g_peer, w_peer_q, peer_keys_a, peer_keys_b, peer_u, peer_v, g_final):
    depth = g_mix.shape[0]
    assert depth == 1
    l = 0
    b, t, _ = x_prompt.shape
    bd, s, _ = x_sample.shape
    n_pool = cache_fox_k.shape[1]
    mem_len = mem_prompt.shape[1]

    in_w = _prep_in_weights(w_in[l])
    ka, kb = _prep_peer_keys(peer_keys_a[l], peer_keys_b[l])
    W = dict(wfo=w_fox_out[l].astype(BF16), wco=w_conv_out[l].astype(BF16), wo=w_o[l].astype(BF16),
             g_xattn=g_xattn[l], wxq=w_xq[l].astype(BF16), wxo=w_xo[l].astype(BF16), g_peer=g_peer[l],
             wpq=w_peer_q[l].astype(BF16), ka=ka, kb=kb, pu=_pack_table(peer_u[l]), pv=_pack_table(peer_v[l]),
             g_final=g_final)

    mk, mv = _memory_kv(mem_prompt.reshape(b * mem_len, D_MODEL), g_mem[l], w_xk[l].astype(BF16),
                        w_xv[l].astype(BF16), tm=PROJ_TILE)
    mk3 = mk.reshape(b, mem_len, D_MODEL)
    mv3 = mv.reshape(b, mem_len, D_MODEL)
    prev0 = jnp.zeros((SUBLANES, CONV_WIDTH), F32)
    kf_rows, vf_rows, logf_rows, utail_rows, y_rows = [], [], [], [], []
    pending = None
    for r in range(b):
        q, kf, vf, kb16, vb16, logf, cbz, sga, sgb, utail = _in_proj(
            x_prompt[r], g_mix[l], *in_w, b_forget[l], w_conv[l], prev0, n_seq=1, seq_len=t, tm=PROJ_TILE)
        frow = _cumsum_time(logf.T[None])
        attn = _fox_prompt(q[None], kb16[None], vb16[None], frow.transpose(0, 2, 1), frow, tq=ATTN_TILE)[0]
        started = _retrieve_start(x_prompt[r], attn, cbz, sga, sgb, mk3[r:r + 1], mv3[r:r + 1], 1, W, ATTN_TILE)
        if pending is not None:
            y_rows.append(_retrieve_finish(pending, W))
        pending = started
        kf_rows.append(kf)
        vf_rows.append(vf)
        logf_rows.append(logf)
        utail_rows.append(utail)
    heads = lambda a, nb, nt: a.reshape(1, nb, nt, FOX_HEADS, FOX_HEAD_DIM)
    conv_state_p = jnp.stack(utail_rows)[:, SUBLANES - (CONV_K - 1):][None]

    xs = x_sample.reshape(bd * s, D_MODEL)
    prev_s = jnp.concatenate([jnp.zeros((bd, SUBLANES - (CONV_K - 1), CONV_WIDTH), F32), state_conv[l]],
                             axis=1).reshape(bd * SUBLANES, CONV_WIDTH)
    qs, kfs, vfs, kbs, vbs, logfs, cbzs, sgas, sgbs, utails = _in_proj(
        xs, g_mix[l], *in_w, b_forget[l], w_conv[l], prev_s, n_seq=bd, seq_len=s, tm=bd * s)
    assert s * FOX_HEADS == SAMPLE_ROWS
    pad_page = lambda a: jnp.pad(a.reshape(bd, s, FOX_WIDTH).transpose(0, 2, 1),
                                 ((0, 0), (0, 0), (0, PAGE_SIZE - s)))
    lf3 = logfs.reshape(bd, s, FOX_HEADS)
    lfcol = lf3.reshape(bd, SAMPLE_ROWS, 1)
    lfrow = jnp.pad(lf3.transpose(0, 2, 1), ((0, 0), (0, 0), (0, LANES - s)))
    page_view = lambda c: c.transpose(0, 2, 3, 1)
    attn_s = _fox_sample(
        page_table, qs.reshape(bd, s, FOX_WIDTH), pad_page(kbs), pad_page(vbs), lfcol, lfrow,
        page_view(cache_fox_k[l]), page_view(cache_fox_v[l]), cache_fox_logf[l].transpose(0, 2, 1),
        pages_per_step=SAMPLE_PAGES_PER_STEP).reshape(bd * s, FOX_WIDTH)
    cmk = cache_mem_k[l].reshape(bd, mem_len, D_MODEL)
    cmv = cache_mem_v[l].reshape(bd, mem_len, D_MODEL)
    started_s = _retrieve_start(xs, attn_s, cbzs, sgas, sgbs, cmk, cmv, bd, W, s)
    y_rows.append(_retrieve_finish(pending, W))
    y_prompt = jnp.stack(y_rows)
    y_sample = _retrieve_finish(started_s, W).reshape(bd, s, D_MODEL)
    conv_state_s = utails.reshape(bd, SUBLANES, CONV_WIDTH)[:, SUBLANES - (CONV_K - 1):][None]

    return (y_prompt, y_sample,
            heads(jnp.stack(kf_rows), b, t), heads(jnp.stack(vf_rows), b, t), jnp.stack(logf_rows)[None],
            mk.reshape(1, b, mem_len, XA_HEADS, XA_HEAD_DIM), mv.reshape(1, b, mem_len, XA_HEADS, XA_HEAD_DIM),
            conv_state_p,
            heads(kfs, bd, s), heads(vfs, bd, s), logfs.reshape(1, bd, s, FOX_HEADS), conv_state_s)
```

```python
import functools

import jax
import jax.numpy as jnp
from jax import lax
from jax.experimental import pallas as pl
from jax.experimental.pallas import tpu as pltpu
from jax.experimental.pallas import tpu_sc as plsc

F32 = jnp.float32
BF16 = jnp.bfloat16
I32 = jnp.int32

D_MODEL = 1024
FOX_HEADS = 8
FOX_HEAD_DIM = 64
FOX_WIDTH = FOX_HEADS * FOX_HEAD_DIM
CONV_WIDTH = D_MODEL // 2
CONV_K = 3
PAGE_SIZE = 128
XA_HEADS = 4
XA_HEAD_DIM = D_MODEL // XA_HEADS
PEER_HEADS = 8
PEER_N_KEYS = 128
PEER_HALF = 64
PEER_TOPK = 16
PEER_SLOTS = PEER_HEADS * PEER_TOPK
PEER_TILE = 128
RMS_EPS = 1e-6

LANES = 128
SUBLANES = 8
VMEM_LIMIT = 56 * 1024 * 1024
NEG_INF = float("-inf")


def _cparams(sem):
    return pltpu.CompilerParams(dimension_semantics=sem, vmem_limit_bytes=VMEM_LIMIT)


def _full(shape):
    return pl.BlockSpec(shape, lambda *_: (0,) * len(shape))


def _rmsnorm(x, g):
    return x * lax.rsqrt(jnp.mean(x * x, axis=-1, keepdims=True) + RMS_EPS) * g


def _log_sigmoid(x):
    return jnp.minimum(x, 0.0) - jnp.log1p(jnp.exp(-jnp.abs(x)))


def _in_proj_body(x_ref, g_ref, wqkv_ref, wf_ref, wc_ref, wg_ref, bf_ref, wconv_ref, prev_ref,
                  q_ref, kf_ref, vf_ref, kb_ref, vb_ref, logf_ref, cbz_ref, sga_ref, sgb_ref, utail_ref,
                  carry_ref, *, seq_rows):
    tm = x_ref.shape[0]
    xn = _rmsnorm(x_ref[...], g_ref[...]).astype(BF16)

    qkv = jnp.dot(xn, wqkv_ref[...], preferred_element_type=F32)
    k = qkv[:, FOX_WIDTH:2 * FOX_WIDTH]
    v = qkv[:, 2 * FOX_WIDTH:]
    q_ref[...] = (qkv[:, :FOX_WIDTH] * (FOX_HEAD_DIM ** -0.5)).astype(BF16)
    kf_ref[...] = k
    vf_ref[...] = v
    kb_ref[...] = k.astype(BF16)
    vb_ref[...] = v.astype(BF16)

    fl = jnp.dot(xn, wf_ref[...], preferred_element_type=F32)
    logf_ref[...] = _log_sigmoid(fl[:, :FOX_HEADS] + bf_ref[...])

    c3 = jnp.dot(xn, wc_ref[...], preferred_element_type=F32)
    u = c3[:, CONV_WIDTH:2 * CONV_WIDTH] * c3[:, :CONV_WIDTH]
    cb = c3[:, 2 * CONV_WIDTH:]
    rows = lax.broadcasted_iota(I32, (tm, CONV_WIDTH), 0)
    if seq_rows is None:
        @pl.when(pl.program_id(1) == 0)
        def _():
            carry_ref[...] = prev_ref[...]
        hist = carry_ref[...]
        u1 = jnp.where(rows == 0, hist[7:8], pltpu.roll(u, 1, 0))
        u2 = jnp.where(rows == 0, hist[6:7], jnp.where(rows == 1, hist[7:8], pltpu.roll(u, 2, 0)))
        carry_ref[...] = u[tm - SUBLANES:]
        utail_ref[...] = u[tm - SUBLANES:]
    else:
        t = rows % seq_rows
        prev = prev_ref[...]
        u1 = jnp.where(t == 0, pltpu.roll(prev, tm - (seq_rows - 1), 0), pltpu.roll(u, 1, 0))
        u2 = jnp.where(t < 2, pltpu.roll(prev, tm - (seq_rows - 2), 0), pltpu.roll(u, 2, 0))
        utail_ref[...] = u
    wconv = wconv_ref[...]
    conv = u2 * wconv[0:1] + u1 * wconv[1:2] + u * wconv[2:3]
    cbz_ref[...] = (cb * conv).astype(BF16)

    gates = jnp.dot(xn, wg_ref[...], preferred_element_type=F32)
    sga_ref[...] = jax.nn.sigmoid(gates[:, :D_MODEL])
    sgb_ref[...] = jax.nn.sigmoid(gates[:, D_MODEL:])


def _prep_in_weights(w_in):
    o_f = 3 * FOX_WIDTH
    o_c = o_f + FOX_HEADS
    o_g = o_c + 3 * CONV_WIDTH
    wqkv = w_in[:, :o_f].astype(BF16)
    wf = jnp.pad(w_in[:, o_f:o_c], ((0, 0), (0, LANES - FOX_HEADS))).astype(BF16)
    wc = w_in[:, o_c:o_g].astype(BF16)
    wg = w_in[:, o_g:].astype(BF16)
    return wqkv, wf, wc, wg


def _in_proj(x2, g_mix, wqkv, wf, wc, wg, b_forget, w_conv, prev8, *, n_seq, seq_len, tm):
    n = x2.shape[0]
    if seq_len >= tm:
        assert seq_len % tm == 0
        nt = seq_len // tm
        grid = (n_seq, nt)
        tok = lambda w: pl.BlockSpec((tm, w), lambda b, t: (b * nt + t, 0))
        seq8 = pl.BlockSpec((SUBLANES, CONV_WIDTH), lambda b, t: (b, 0))
        seq_rows = None
    else:
        assert seq_len == SUBLANES and tm == n
        grid = (1, 1)
        tok = lambda w: pl.BlockSpec((tm, w), lambda b, t: (0, 0))
        seq8 = pl.BlockSpec((tm, CONV_WIDTH), lambda b, t: (0, 0))
        seq_rows = seq_len
    out_shape = (
        jax.ShapeDtypeStruct((n, FOX_WIDTH), BF16),
        jax.ShapeDtypeStruct((n, FOX_WIDTH), F32),
        jax.ShapeDtypeStruct((n, FOX_WIDTH), F32),
        jax.ShapeDtypeStruct((n, FOX_WIDTH), BF16),
        jax.ShapeDtypeStruct((n, FOX_WIDTH), BF16),
        jax.ShapeDtypeStruct((n, FOX_HEADS), F32),
        jax.ShapeDtypeStruct((n, CONV_WIDTH), BF16),
        jax.ShapeDtypeStruct((n, D_MODEL), F32),
        jax.ShapeDtypeStruct((n, D_MODEL), F32),
        jax.ShapeDtypeStruct((n_seq * SUBLANES, CONV_WIDTH), F32),
    )
    out_specs = (tok(FOX_WIDTH), tok(FOX_WIDTH), tok(FOX_WIDTH), tok(FOX_WIDTH), tok(FOX_WIDTH),
                 tok(FOX_HEADS), tok(CONV_WIDTH), tok(D_MODEL), tok(D_MODEL), seq8)
    in_specs = [tok(D_MODEL), _full((1, D_MODEL)), _full(wqkv.shape), _full(wf.shape), _full(wc.shape),
                _full(wg.shape), _full((1, FOX_HEADS)), _full((CONV_K, CONV_WIDTH)), seq8]
    return pl.pallas_call(
        functools.partial(_in_proj_body, seq_rows=seq_rows),
        grid=grid, in_specs=in_specs, out_specs=out_specs, out_shape=out_shape,
        scratch_shapes=[pltpu.VMEM((SUBLANES, CONV_WIDTH), F32)],
        compiler_params=_cparams(("arbitrary", "arbitrary")),
        name="in_proj",
    )(x2, g_mix.reshape(1, D_MODEL), wqkv, wf, wc, wg, b_forget.reshape(1, FOX_HEADS), w_conv, prev8)


def _lane_prefix_scan(x, lane):
    s = x
    sh = 1
    while sh < LANES:
        s = s + jnp.where(lane >= sh, pltpu.roll(s, sh, 1), 0.0)
        sh *= 2
    return s


def _cumsum_body(x_ref, o_ref):
    n_chunks = x_ref.shape[2] // LANES
    lane = lax.broadcasted_iota(I32, (FOX_HEADS, LANES), 1)

    def step(c, carry):
        off = pl.multiple_of(c * LANES, LANES)
        s = _lane_prefix_scan(x_ref[0, :, pl.ds(off, LANES)], lane) + carry
        o_ref[0, :, pl.ds(off, LANES)] = s
        return s[:, LANES - 1:]

    lax.fori_loop(0, n_chunks, step, jnp.zeros((FOX_HEADS, 1), F32))


def _cumsum_time(logf_t):
    b, h, t = logf_t.shape
    spec = pl.BlockSpec((1, h, t), lambda i: (i, 0, 0))
    return pl.pallas_call(
        _cumsum_body, grid=(b,), in_specs=[spec], out_specs=spec,
        out_shape=jax.ShapeDtypeStruct(logf_t.shape, F32),
        compiler_params=_cparams(("arbitrary",)), name="forget_cumsum",
    )(logf_t)


def _fox_prompt_body(q_ref, k_ref, v_ref, fcol_ref, frow_ref, o_ref, *, tq):
    hp = pl.program_id(1)
    qi = pl.program_id(2)
    q2 = q_ref[0]
    lane = lax.broadcasted_iota(I32, (tq, LANES), 1)
    upper = lane >= FOX_HEAD_DIM
    col8 = lax.broadcasted_iota(I32, (tq, FOX_HEADS), 1)
    fcol8 = fcol_ref[0]
    rowi = lax.broadcasted_iota(I32, (tq, tq), 0)
    coli = lax.broadcasted_iota(I32, (tq, tq), 1)
    zero = jnp.zeros_like(q2)

    outs = []
    for hh in range(2):
        h = 2 * hp + hh
        qh = jnp.where(upper, q2, zero) if hh else jnp.where(upper, zero, q2)
        fq = jnp.sum(jnp.where(col8 == h, fcol8, 0.0), axis=1, keepdims=True)

        def block(kb, carry, masked):
            m, l, acc = carry
            off = pl.multiple_of(kb * tq, tq)
            kk = k_ref[0, pl.ds(off, tq), :]
            vv = v_ref[0, pl.ds(off, tq), :]
            fk = frow_ref[0, pl.ds(h, 1), pl.ds(off, tq)]
            s = lax.dot_general(qh, kk, (((1,), (1,)), ((), ())), preferred_element_type=F32)
            s = s + fq - fk
            if masked:
                s = jnp.where(coli <= rowi, s, NEG_INF)
            m_new = jnp.maximum(m, jnp.max(s, axis=1, keepdims=True))
            p = jnp.exp(s - m_new)
            alpha = jnp.exp(m - m_new)
            l = alpha * l + jnp.sum(p, axis=1, keepdims=True)
            acc = alpha * acc + jnp.dot(p.astype(BF16), vv, preferred_element_type=F32)
            return m_new, l, acc

        init = (jnp.full((tq, 1), NEG_INF, F32), jnp.zeros((tq, 1), F32), jnp.zeros((tq, LANES), F32))
        carry = lax.fori_loop(0, qi, functools.partial(block, masked=False), init)
        m, l, acc = block(qi, carry, True)
        outs.append(acc / l)
    o_ref[0] = jnp.where(upper, outs[1], outs[0]).astype(o_ref.dtype)


def _fox_prompt(q, k, v, fcol, frow, *, tq):
    b, t, _ = q.shape
    n_pairs = FOX_WIDTH // LANES
    return pl.pallas_call(
        functools.partial(_fox_prompt_body, tq=tq),
        grid=(b, n_pairs, t // tq),
        in_specs=[
            pl.BlockSpec((1, tq, LANES), lambda i, p, j: (i, j, p)),
            pl.BlockSpec((1, t, LANES), lambda i, p, j: (i, 0, p)),
            pl.BlockSpec((1, t, LANES), lambda i, p, j: (i, 0, p)),
            pl.BlockSpec((1, tq, FOX_HEADS), lambda i, p, j: (i, j, 0)),
            pl.BlockSpec((1, FOX_HEADS, t), lambda i, p, j: (i, 0, 0)),
        ],
        out_specs=pl.BlockSpec((1, tq, LANES), lambda i, p, j: (i, j, p)),
        out_shape=jax.ShapeDtypeStruct((b, t, FOX_WIDTH), BF16),
        compiler_params=_cparams(("arbitrary", "arbitrary", "arbitrary")),
        name="fox_prompt",
    )(q, k, v, fcol, frow)


SAMPLE_ROWS = 64


def _lane_suffix_excl(x, lane):
    s = jnp.where(lane < LANES - 1, pltpu.roll(x, LANES - 1, 1), 0.0)
    sh = 1
    while sh < LANES:
        s = s + jnp.where(lane < LANES - sh, pltpu.roll(s, LANES - sh, 1), 0.0)
        sh *= 2
    return s


def _fox_sample_body(pt_ref, q_ref, kn_ref, vn_ref, lfcol_ref, lfrow_ref, *rest, pages_per_step):
    pg = pages_per_step
    k_refs = rest[:pg]
    v_refs = rest[pg:2 * pg]
    lf_refs = rest[2 * pg:3 * pg]
    o_ref = rest[3 * pg]
    m_ref, l_ref, acc_ref, tot_ref, qbd_ref, rt_ref = rest[3 * pg + 1:]
    g = pl.program_id(1)
    lane8 = lax.broadcasted_iota(I32, (FOX_HEADS, LANES), 1)
    row = lax.broadcasted_iota(I32, (SAMPLE_ROWS, FOX_WIDTH), 0)
    lane = lax.broadcasted_iota(I32, (SAMPLE_ROWS, FOX_WIDTH), 1)
    head_mask = (lane // FOX_HEAD_DIM) == (row % FOX_HEADS)
    nt = (((1,), (1,)), ((), ()))

    def tile8(x):
        return jnp.concatenate([x] * (SAMPLE_ROWS // FOX_HEADS), axis=0)

    @pl.when(g == 0)
    def _():
        q = q_ref[0]
        qrows = jnp.concatenate(
            [jnp.broadcast_to(q[t:t + 1, :], (FOX_HEADS, FOX_WIDTH)) for t in range(q.shape[0])], axis=0)
        qbd = jnp.where(head_mask, qrows, jnp.zeros_like(qrows))
        qbd_ref[...] = qbd
        r1 = lax.broadcasted_iota(I32, (SAMPLE_ROWS, 1), 0)
        x = lfcol_ref[0]
        s = jnp.where(r1 < SAMPLE_ROWS - 8, pltpu.roll(x, SAMPLE_ROWS - 8, 0), 0.0)
        for sh in (8, 16, 32):
            s = s + jnp.where(r1 < SAMPLE_ROWS - sh, pltpu.roll(s, SAMPLE_ROWS - sh, 0), 0.0)
        rt = -s
        rt_ref[...] = rt
        lfn = lfrow_ref[0]
        excl = _lane_suffix_excl(lfn, lane8)
        tot_ref[...] = excl[:, 0:1] + lfn[:, 0:1]
        sc = jnp.dot(qbd, kn_ref[0], preferred_element_type=F32) + rt + tile8(excl)
        rr = lax.broadcasted_iota(I32, (SAMPLE_ROWS, LANES), 0)
        cc = lax.broadcasted_iota(I32, (SAMPLE_ROWS, LANES), 1)
        sc = jnp.where(cc <= rr // FOX_HEADS, sc, NEG_INF)
        m_new = jnp.max(sc, axis=1, keepdims=True)
        p = jnp.exp(sc - m_new)
        m_ref[...] = m_new
        l_ref[...] = jnp.sum(p, axis=1, keepdims=True)
        acc_ref[...] = lax.dot_general(p.astype(BF16), vn_ref[0], nt, preferred_element_type=F32)

    qbd = qbd_ref[...]
    tot = tot_ref[...]
    biases, ks, vs = [], [], []
    for i in reversed(range(pg)):
        lfp = lf_refs[i][0]
        r = _lane_suffix_excl(lfp, lane8) + tot
        tot = r[:, 0:1] + lfp[:, 0:1]
        biases.append(tile8(r))
        ks.append(k_refs[i][0].reshape(FOX_WIDTH, PAGE_SIZE).astype(BF16))
        vs.append(v_refs[i][0].reshape(FOX_WIDTH, PAGE_SIZE).astype(BF16))
    tot_ref[...] = tot
    sc = jnp.dot(qbd, jnp.concatenate(ks, axis=1), preferred_element_type=F32)
    sc = sc + rt_ref[...] + jnp.concatenate(biases, axis=1)
    m_old = m_ref[...]
    m_new = jnp.maximum(m_old, jnp.max(sc, axis=1, keepdims=True))
    p = jnp.exp(sc - m_new)
    alpha = jnp.exp(m_old - m_new)
    l_ref[...] = alpha * l_ref[...] + jnp.sum(p, axis=1, keepdims=True)
    acc_ref[...] = alpha * acc_ref[...] + lax.dot_general(
        p.astype(BF16), jnp.concatenate(vs, axis=1), nt, preferred_element_type=F32)
    m_ref[...] = m_new

    @pl.when(g == pl.num_programs(1) - 1)
    def _():
        full = jnp.where(head_mask, acc_ref[...] / l_ref[...], 0.0)
        o_ref[0] = jnp.concatenate(
            [jnp.sum(full[FOX_HEADS * t:FOX_HEADS * (t + 1)], axis=0, keepdims=True)
             for t in range(SAMPLE_ROWS // FOX_HEADS)], axis=0).astype(o_ref.dtype)


def _fox_sample(page_table, q, kn_t, vn_t, lfcol, lfrow_pad, cache_kt, cache_vt, cache_lf_t, *, pages_per_step):
    bd, n_pages = page_table.shape
    pg = pages_per_step
    assert n_pages % pg == 0
    ng = n_pages // pg
    s_new = q.shape[1]

    def page_spec(shape, i):
        zeros = (0,) * (len(shape) - 1)
        return pl.BlockSpec(shape, lambda b, g, pt: (pt[b, (ng - 1 - g) * pg + i],) + zeros)

    req = lambda shape: pl.BlockSpec(shape, lambda b, g, pt: (b, 0, 0))
    in_specs = [req((1, s_new, FOX_WIDTH)), req((1, FOX_WIDTH, PAGE_SIZE)), req((1, FOX_WIDTH, PAGE_SIZE)),
                req((1, SAMPLE_ROWS, 1)), req((1, FOX_HEADS, LANES))]
    in_specs += [page_spec((1, FOX_HEADS, FOX_HEAD_DIM, PAGE_SIZE), i) for i in range(pg)]
    in_specs += [page_spec((1, FOX_HEADS, FOX_HEAD_DIM, PAGE_SIZE), i) for i in range(pg)]
    in_specs += [page_spec((1, FOX_HEADS, PAGE_SIZE), i) for i in range(pg)]
    grid_spec = pltpu.PrefetchScalarGridSpec(
        num_scalar_prefetch=1, grid=(bd, ng), in_specs=in_specs,
        out_specs=req((1, s_new, FOX_WIDTH)),
        scratch_shapes=[pltpu.VMEM((SAMPLE_ROWS, 1), F32), pltpu.VMEM((SAMPLE_ROWS, 1), F32),
                        pltpu.VMEM((SAMPLE_ROWS, FOX_WIDTH), F32), pltpu.VMEM((FOX_HEADS, 1), F32),
                        pltpu.VMEM((SAMPLE_ROWS, FOX_WIDTH), BF16), pltpu.VMEM((SAMPLE_ROWS, 1), F32)])
    return pl.pallas_call(
        functools.partial(_fox_sample_body, pages_per_step=pg),
        grid_spec=grid_spec, out_shape=jax.ShapeDtypeStruct((bd, s_new, FOX_WIDTH), BF16),
        compiler_params=_cparams(("arbitrary", "arbitrary")), name="fox_sample",
    )(page_table, q, kn_t, vn_t, lfcol, lfrow_pad, *([cache_kt] * pg), *([cache_vt] * pg), *([cache_lf_t] * pg))


def _merge_body(x_ref, attn_ref, cbz_ref, sga_ref, sgb_ref, wfo_ref, wco_ref, wo_ref, gx_ref, wxq_ref,
                h1_ref, qx_ref):
    ya = jnp.dot(attn_ref[...], wfo_ref[...], preferred_element_type=F32)
    yb = jnp.dot(cbz_ref[...], wco_ref[...], preferred_element_type=F32)
    mix = (sga_ref[...] * ya + sgb_ref[...] * yb).astype(BF16)
    h1 = x_ref[...] + jnp.dot(mix, wo_ref[...], preferred_element_type=F32)
    h1_ref[...] = h1
    xn = _rmsnorm(h1, gx_ref[...]).astype(BF16)
    qx_ref[...] = (jnp.dot(xn, wxq_ref[...], preferred_element_type=F32) * (XA_HEAD_DIM ** -0.5)).astype(BF16)


def _merge(x2, attn, cbz, sga, sgb, wfo, wco, wo, g_xattn, wxq, *, tm):
    n = x2.shape[0]
    tok = lambda w: pl.BlockSpec((tm, w), lambda i: (i, 0))
    return pl.pallas_call(
        _merge_body, grid=(n // tm,),
        in_specs=[tok(D_MODEL), tok(FOX_WIDTH), tok(CONV_WIDTH), tok(D_MODEL), tok(D_MODEL),
                  _full(wfo.shape), _full(wco.shape), _full(wo.shape), _full((1, D_MODEL)), _full(wxq.shape)],
        out_specs=(tok(D_MODEL), tok(D_MODEL)),
        out_shape=(jax.ShapeDtypeStruct((n, D_MODEL), F32), jax.ShapeDtypeStruct((n, D_MODEL), BF16)),
        compiler_params=_cparams(("arbitrary",)), name="merge",
    )(x2, attn, cbz, sga, sgb, wfo, wco, wo, g_xattn.reshape(1, D_MODEL), wxq)


def _memory_kv_body(m_ref, g_ref, wk_ref, wv_ref, mk_ref, mv_ref):
    mn = _rmsnorm(m_ref[...], g_ref[...]).astype(BF16)
    mk_ref[...] = jnp.dot(mn, wk_ref[...], preferred_element_type=F32)
    mv_ref[...] = jnp.dot(mn, wv_ref[...], preferred_element_type=F32)


def _memory_kv(mem2, g_mem, wxk, wxv, *, tm):
    n = mem2.shape[0]
    tok = pl.BlockSpec((tm, D_MODEL), lambda i: (i, 0))
    return pl.pallas_call(
        _memory_kv_body, grid=(n // tm,),
        in_specs=[tok, _full((1, D_MODEL)), _full(wxk.shape), _full(wxv.shape)],
        out_specs=(tok, tok),
        out_shape=(jax.ShapeDtypeStruct((n, D_MODEL), F32),) * 2,
        compiler_params=_cparams(("arbitrary",)), name="memory_kv",
    )(mem2, g_mem.reshape(1, D_MODEL), wxk, wxv)


def _cross_attn_body(q_ref, mk_ref, mv_ref, o_ref):
    q = q_ref[0]
    outs = []
    for h in range(XA_HEADS):
        sl = slice(h * XA_HEAD_DIM, (h + 1) * XA_HEAD_DIM)
        kh = mk_ref[0, :, sl].astype(BF16)
        vh = mv_ref[0, :, sl].astype(BF16)
        s = lax.dot_general(q[:, sl], kh, (((1,), (1,)), ((), ())), preferred_element_type=F32)
        p = jnp.exp(s - jnp.max(s, axis=1, keepdims=True))
        p = p / jnp.sum(p, axis=1, keepdims=True)
        outs.append(jnp.dot(p.astype(BF16), vh, preferred_element_type=F32))
    o_ref[0] = jnp.concatenate(outs, axis=1).astype(o_ref.dtype)


def _cross_attn(qx, mk, mv, *, tm):
    b, t, _ = qx.shape
    m = mk.shape[1]
    tok = pl.BlockSpec((1, tm, D_MODEL), lambda i, j: (i, j, 0))
    mem = pl.BlockSpec((1, m, D_MODEL), lambda i, j: (i, 0, 0))
    return pl.pallas_call(
        _cross_attn_body, grid=(b, t // tm), in_specs=[tok, mem, mem], out_specs=tok,
        out_shape=jax.ShapeDtypeStruct((b, t, D_MODEL), BF16),
        compiler_params=_cparams(("arbitrary", "arbitrary")), name="cross_attn",
    )(qx, mk, mv)


PEER_PAIRS = tuple((i, j) for i in range(PEER_TOPK) for j in range(PEER_TOPK) if (i + 1) * (j + 1) <= PEER_TOPK)
PEER_CAND_ROWS = -(-len(PEER_PAIRS) // SUBLANES) * SUBLANES


def _top_rows(s, k, payload=None):
    n = s.shape[0]
    iota = lax.broadcasted_iota(I32, s.shape, 0)
    vals, ids = [], []
    for _ in range(k):
        m = jnp.max(s, axis=0, keepdims=True)
        r = jnp.min(jnp.where(s == m, iota, n), axis=0, keepdims=True)
        hit = iota == r
        vals.append(m)
        ids.append(r if payload is None else jnp.max(jnp.where(hit, payload, -1), axis=0, keepdims=True))
        s = jnp.where(hit, NEG_INF, s)
    return vals, ids


def _peer_front_body(h1_ref, o_ref, wxo_ref, gp_ref, wpq_ref, ka_ref, kb_ref,
                     h2_ref, xn_ref, eidx_ref, etok_ref, gate_ref, qp_ref, e_ref, g_ref):
    tm = h1_ref.shape[0]
    h2 = h1_ref[...] + jnp.dot(o_ref[...], wxo_ref[...], preferred_element_type=F32)
    h2_ref[...] = h2
    xn = _rmsnorm(h2, gp_ref[...]).astype(BF16)
    xn_ref[...] = xn.astype(F32)
    qp_ref[...] = jnp.dot(xn, wpq_ref[...], preferred_element_type=F32).astype(BF16)

    def head(h, _):
        qh = qp_ref[:, pl.ds(pl.multiple_of(h * LANES, LANES), LANES)]
        nt = (((1,), (1,)), ((), ()))
        sa = lax.dot_general(ka_ref[h], qh, nt, preferred_element_type=F32)
        sb = lax.dot_general(kb_ref[h], qh, nt, preferred_element_type=F32)
        va, ia = _top_rows(sa, PEER_TOPK)
        vb, ib = _top_rows(sb, PEER_TOPK)
        pad = PEER_CAND_ROWS - len(PEER_PAIRS)
        cand = jnp.concatenate([va[i] + vb[j] for i, j in PEER_PAIRS]
                               + [jnp.full((pad, tm), NEG_INF, F32)], axis=0)
        ceid = jnp.concatenate([ia[i] * PEER_N_KEYS + ib[j] for i, j in PEER_PAIRS]
                               + [jnp.zeros((pad, tm), I32)], axis=0)
        ts, te = _top_rows(cand, PEER_TOPK, payload=ceid)
        ex = [jnp.exp(t - ts[0]) for t in ts]
        den = ex[0]
        for e in ex[1:]:
            den = den + e
        row0 = pl.multiple_of(h * PEER_TOPK, PEER_TOPK)
        e_ref[pl.ds(row0, PEER_TOPK), :] = jnp.concatenate(te, axis=0)
        g_ref[pl.ds(row0, PEER_TOPK), :] = jnp.concatenate([e / den for e in ex], axis=0)
        return 0

    lax.fori_loop(0, PEER_HEADS, head, 0)
    for c in range(eidx_ref.shape[0]):
        eidx_ref[c] = e_ref[:, c * PEER_TILE:(c + 1) * PEER_TILE]
    etok_ref[...] = e_ref[...].T
    gate_ref[...] = g_ref[...].T


def _prep_peer_keys(keys_a, keys_b):
    z = jnp.zeros_like(keys_a)
    return (jnp.concatenate([keys_a, z], axis=-1).astype(BF16), jnp.concatenate([z, keys_b], axis=-1).astype(BF16))


def _peer_front(h1, o, wxo, g_peer, wpq, ka_pad, kb_pad, *, tm):
    n = h1.shape[0]
    assert tm % PEER_TILE == 0
    sub = tm // PEER_TILE
    tok = lambda w: pl.BlockSpec((tm, w), lambda i: (i, 0))
    return pl.pallas_call(
        _peer_front_body, grid=(n // tm,),
        in_specs=[tok(D_MODEL), tok(D_MODEL), _full(wxo.shape), _full((1, D_MODEL)), _full(wpq.shape),
                  _full(ka_pad.shape), _full(kb_pad.shape)],
        out_specs=(tok(D_MODEL), tok(D_MODEL), pl.BlockSpec((sub, PEER_SLOTS, PEER_TILE), lambda i: (i, 0, 0)),
                   tok(PEER_SLOTS), tok(PEER_SLOTS)),
        out_shape=(jax.ShapeDtypeStruct((n, D_MODEL), F32), jax.ShapeDtypeStruct((n, D_MODEL), F32),
                   jax.ShapeDtypeStruct((n // PEER_TILE, PEER_SLOTS, PEER_TILE), I32),
                   jax.ShapeDtypeStruct((n, PEER_SLOTS), I32),
                   jax.ShapeDtypeStruct((n, PEER_SLOTS), F32)),
        scratch_shapes=[pltpu.VMEM((tm, PEER_HEADS * LANES), BF16), pltpu.VMEM((PEER_SLOTS, tm), I32),
                        pltpu.VMEM((PEER_SLOTS, tm), F32)],
        compiler_params=_cparams(("arbitrary",)), name="peer_front",
    )(h1, o, wxo, g_peer.reshape(1, D_MODEL), wpq, ka_pad, kb_pad)


PACKED_WIDTH = D_MODEL // 2


def _pack_table(t):
    b = lax.bitcast_convert_type(t.astype(BF16), jnp.uint16).astype(jnp.uint32)
    return lax.bitcast_convert_type(b[:, :PACKED_WIDTH] | (b[:, PACKED_WIDTH:] << 16), I32)


def _unpack(w):
    lo = lax.bitcast_convert_type(w << 16, F32)
    hi = lax.bitcast_convert_type(w & jnp.int32(-65536), F32)
    return lo, hi


SC_CORES = 2
SC_SUBCORES = 16
SC_WINDOW = 64


def _sc_gather2(table_u, table_v, idx):
    r = idx.shape[0]
    workers = SC_CORES * SC_SUBCORES
    assert r % (workers * SC_WINDOW) == 0
    per_worker = r // workers
    n_win = per_worker // SC_WINDOW
    width = table_u.shape[1]
    mesh = plsc.VectorSubcoreMesh(core_axis_name="c", subcore_axis_name="s")
    out = jax.ShapeDtypeStruct((r, width), table_u.dtype)

    @functools.partial(
        pl.kernel, mesh=mesh, out_type=(out, out),
        scratch_types=[pltpu.VMEM((SC_WINDOW,), I32), pltpu.VMEM((SC_WINDOW, width), table_u.dtype),
                       pltpu.VMEM((SC_WINDOW, width), table_u.dtype),
                       pltpu.SemaphoreType.DMA, pltpu.SemaphoreType.DMA])
    def gather(u_hbm, v_hbm, idx_hbm, ou_hbm, ov_hbm, idx_v, ru_v, rv_v, sem_u, sem_v):
        wid = lax.axis_index("s") * SC_CORES + lax.axis_index("c")
        base = wid * per_worker

        @pl.loop(0, n_win)
        def _(i):
            off = base + i * SC_WINDOW
            pltpu.sync_copy(idx_hbm.at[pl.ds(off, SC_WINDOW)], idx_v)
            cu = pltpu.async_copy(u_hbm.at[idx_v], ru_v, sem_u)
            cv = pltpu.async_copy(v_hbm.at[idx_v], rv_v, sem_v)
            cu.wait()
            pltpu.sync_copy(ru_v, ou_hbm.at[pl.ds(off, SC_WINDOW)])
            cv.wait()
            pltpu.sync_copy(rv_v, ov_hbm.at[pl.ds(off, SC_WINDOW)])

    return gather(table_u, table_v, idx)


def _peer_back_body(ug_ref, vg_ref, xn_ref, gate_ref, h2_ref, gf_ref, y_ref, x32_ref, out_ref, *, slots_per_step):
    sg = pl.program_id(1)
    tp = xn_ref.shape[0]
    n_groups = tp // SUBLANES
    lane = lax.broadcasted_iota(I32, (SUBLANES, PEER_SLOTS), 1)
    slot0 = sg * slots_per_step

    @pl.when(sg == 0)
    def _():
        x32_ref[...] = xn_ref[...].astype(F32)
        out_ref[...] = jnp.zeros_like(out_ref)

    def rows_of(r):
        return pl.ds(pl.multiple_of(r * SUBLANES, SUBLANES), SUBLANES)

    def u_pass(r):
        rows = rows_of(r)
        xlo = x32_ref[rows, :PACKED_WIDTH]
        xhi = x32_ref[rows, PACKED_WIDTH:]
        act = jnp.zeros((SUBLANES, PEER_SLOTS), F32)
        for j in range(slots_per_step):
            ulo, uhi = _unpack(ug_ref[0, j, rows, :])
            a = jnp.sum(ulo * xlo + uhi * xhi, axis=1, keepdims=True)
            act = jnp.where(lane == slot0 + j, a, act)
        return gate_ref[rows, :] * jax.nn.gelu(act)

    def v_pass(r, wts):
        rows = rows_of(r)
        olo = out_ref[rows, :PACKED_WIDTH]
        ohi = out_ref[rows, PACKED_WIDTH:]
        for j in range(slots_per_step):
            w = jnp.sum(jnp.where(lane == slot0 + j, wts, 0.0), axis=1, keepdims=True)
            vlo, vhi = _unpack(vg_ref[0, j, rows, :])
            olo = olo + w * vlo
            ohi = ohi + w * vhi
        out_ref[rows, :PACKED_WIDTH] = olo
        out_ref[rows, PACKED_WIDTH:] = ohi

    def group(r, wts):
        nxt = u_pass(r)
        v_pass(r - 1, wts)
        return nxt

    last = lax.fori_loop(1, n_groups, group, u_pass(0))
    v_pass(n_groups - 1, last)

    @pl.when(sg == pl.num_programs(1) - 1)
    def _():
        y_ref[...] = _rmsnorm(h2_ref[...] + out_ref[...], gf_ref[...])


SC_ROWS = 16
SC_U_BUFS = 4
SC_LANES = 16
PART_WIDTH = PEER_SLOTS * SC_LANES


def _sc_peer(table_u, table_v, idx_v, idx_u, x32):
    n = x32.shape[0]
    workers = SC_CORES * SC_SUBCORES
    tw = n // workers
    assert n % (2 * workers) == 0
    wins = PEER_SLOTS // SC_ROWS
    assert wins % SC_U_BUFS == 0
    words = SC_ROWS * SC_LANES
    n_col = PACKED_WIDTH // SC_LANES
    mesh = plsc.VectorSubcoreMesh(core_axis_name="c", subcore_axis_name="s")
    dma = pltpu.SemaphoreType.DMA

    @functools.partial(
        pl.kernel, mesh=mesh, compiler_params=pltpu.CompilerParams(needs_layout_passes=False),
        out_type=(jax.ShapeDtypeStruct((n * PEER_SLOTS, PACKED_WIDTH), I32),
                  jax.ShapeDtypeStruct((n, PART_WIDTH), F32)),
        scratch_types=[pltpu.VMEM((tw * PEER_SLOTS,), I32), pltpu.VMEM((tw * PEER_SLOTS,), I32),
                       pltpu.VMEM((SC_U_BUFS, SC_ROWS), I32), pltpu.VMEM((2, SC_ROWS), I32),
                       pltpu.VMEM((SC_U_BUFS, SC_ROWS, PACKED_WIDTH), I32),
                       pltpu.VMEM((2, SC_ROWS, PACKED_WIDTH), I32),
                       pltpu.VMEM((2, D_MODEL), F32), pltpu.VMEM((2, PART_WIDTH), F32)] + [dma] * (SC_U_BUFS + 8))
    def body(u_hbm, v_hbm, idxv_hbm, idxu_hbm, x_hbm, ov_hbm, part_hbm,
             idxu_all, idxv_all, iu, iv, ru, rv, xv, pv, *sems):
        su = sems[:SC_U_BUFS]
        sv, sw, sx, sp = (sems[SC_U_BUFS + 2 * q:SC_U_BUFS + 2 * q + 2] for q in range(4))
        ahead = SC_U_BUFS - 1
        wid = lax.axis_index("s") * SC_CORES + lax.axis_index("c")
        tok0 = wid * tw
        row0 = tok0 * PEER_SLOTS
        pltpu.sync_copy(idxu_hbm.at[pl.ds(row0, tw * PEER_SLOTS)], idxu_all)
        pltpu.sync_copy(idxv_hbm.at[pl.ds(row0, tw * PEER_SLOTS)], idxv_all)

        def u_gather(b):
            return pltpu.make_async_copy(u_hbm.at[iu.at[b]], ru.at[b], su[b])

        def v_gather(b):
            return pltpu.make_async_copy(v_hbm.at[iv.at[b]], rv.at[b], sv[b])

        def v_write(b, k):
            return pltpu.make_async_copy(rv.at[b], ov_hbm.at[pl.ds(row0 + k * SC_ROWS, SC_ROWS)], sw[b])

        def x_load(par, i):
            return pltpu.make_async_copy(x_hbm.at[tok0 + i], xv.at[par], sx[par])

        def part_write(par, i):
            return pltpu.make_async_copy(pv.at[par], part_hbm.at[tok0 + i], sp[par])

        def start_u(b, k):
            iu[b, :] = idxu_all[pl.ds(pl.multiple_of(k * SC_ROWS, SC_ROWS), SC_ROWS)]
            u_gather(b).start()

        def start_v(b, k):
            iv[b, :] = idxv_all[pl.ds(pl.multiple_of(k * SC_ROWS, SC_ROWS), SC_ROWS)]
            v_gather(b).start()

        def dots(b, par, w):
            for s in range(SC_ROWS):
                pv[par, pl.ds(w * words + s * SC_LANES, SC_LANES)] = jnp.zeros((SC_LANES,), F32)

            @pl.loop(0, n_col // 2)
            def _(j):
                vals = []
                for h in range(2):
                    c = pl.multiple_of((2 * j + h) * SC_LANES, SC_LANES)
                    xlo = xv[par, pl.ds(c, SC_LANES)]
                    xhi = xv[par, pl.ds(PACKED_WIDTH + c, SC_LANES)]
                    rows = [ru[b, s, pl.ds(c, SC_LANES)] for s in range(SC_ROWS)]
                    vals.append([plsc.bitcast(r << 16, F32) * xlo + plsc.bitcast(r & jnp.int32(-65536), F32) * xhi
                                 for r in rows])
                for s in range(SC_ROWS):
                    plsc.addupdate(pv.at[par, pl.ds(w * words + s * SC_LANES, SC_LANES)], vals[0][s] + vals[1][s])

        x_load(0, 0).start()
        for k0 in range(ahead):
            start_u(k0, k0)
        start_v(0, 0)

        @pl.loop(0, tw // 2)
        def _(i2):
            for par in range(2):
                i = 2 * i2 + par
                more_tokens = i2 < tw // 2 - 1 if par else None
                if par == 0:
                    x_load(1, i + 1).start()
                else:
                    @pl.when(more_tokens)
                    def _():
                        x_load(0, i + 1).start()
                x_load(par, i).wait()

                @pl.when(i2 >= 1)
                def _():
                    part_write(par, i).wait()

                for w in range(wins):
                    b, nb = w % 2, 1 - w % 2
                    bu = w % SC_U_BUFS
                    k = i * wins + w
                    if w >= 1 or par == 1:
                        v_write(nb, k).wait()
                    else:
                        @pl.when(i2 >= 1)
                        def _():
                            v_write(nb, k).wait()
                    if w < wins - 1 or par == 0:
                        start_v(nb, k + 1)
                    else:
                        @pl.when(more_tokens)
                        def _():
                            start_v(nb, k + 1)
                    if w < wins - ahead or par == 0:
                        start_u((w + ahead) % SC_U_BUFS, k + ahead)
                    else:
                        @pl.when(more_tokens)
                        def _():
                            start_u((w + ahead) % SC_U_BUFS, k + ahead)
                    u_gather(bu).wait()
                    dots(bu, par, w)
                    v_gather(b).wait()
                    v_write(b, k).start()
                part_write(par, i).start()

        v_write(1, 0).wait()
        part_write(0, 0).wait()
        part_write(1, 0).wait()

    return body(table_u, table_v, idx_v, idx_u, x32)


def _peer_back2_body(vg_ref, part_ref, ones_ref, gate_ref, h2_ref, gf_ref, y_ref, wts_ref, out_ref, *, slots_per_step):
    sg = pl.program_id(1)
    tp = gate_ref.shape[0]
    lane = lax.broadcasted_iota(I32, (SUBLANES, PEER_SLOTS), 1)
    slot0 = sg * slots_per_step

    @pl.when(sg == 0)
    def _():
        act = jnp.dot(part_ref[...], ones_ref[...], precision=lax.Precision.HIGHEST, preferred_element_type=F32)
        wts_ref[...] = gate_ref[...] * jax.nn.gelu(act)
        out_ref[...] = jnp.zeros_like(out_ref)

    def group(r, _):
        rows = pl.ds(pl.multiple_of(r * SUBLANES, SUBLANES), SUBLANES)
        wts = wts_ref[rows, :]
        olo = out_ref[rows, :PACKED_WIDTH]
        ohi = out_ref[rows, PACKED_WIDTH:]
        for j in range(slots_per_step):
            w = jnp.sum(jnp.where(lane == slot0 + j, wts, 0.0), axis=1, keepdims=True)
            vlo, vhi = _unpack(vg_ref[0, j, rows, :])
            olo = olo + w * vlo
            ohi = ohi + w * vhi
        out_ref[rows, :PACKED_WIDTH] = olo
        out_ref[rows, PACKED_WIDTH:] = ohi
        return 0

    lax.fori_loop(0, tp // SUBLANES, group, 0)

    @pl.when(sg == pl.num_programs(1) - 1)
    def _():
        y_ref[...] = _rmsnorm(h2_ref[...] + out_ref[...], gf_ref[...])


def _peer_back2(vg, part, gate, h2, g_final, *, tp, slots_per_step):
    n = gate.shape[0]
    n_sg = PEER_SLOTS // slots_per_step
    ones = (jnp.arange(PART_WIDTH)[:, None] // SC_LANES == jnp.arange(PEER_SLOTS)[None, :]).astype(F32)
    rows = pl.BlockSpec((1, slots_per_step, tp, PACKED_WIDTH), lambda i, s: (i, s, 0, 0))
    tok = lambda w: pl.BlockSpec((tp, w), lambda i, s: (i, 0))
    return pl.pallas_call(
        functools.partial(_peer_back2_body, slots_per_step=slots_per_step),
        grid=(n // tp, n_sg),
        in_specs=[rows, tok(PART_WIDTH), _full(ones.shape), tok(PEER_SLOTS), tok(D_MODEL), _full((1, D_MODEL))],
        out_specs=tok(D_MODEL),
        out_shape=jax.ShapeDtypeStruct((n, D_MODEL), F32),
        scratch_shapes=[pltpu.VMEM((tp, PEER_SLOTS), F32), pltpu.VMEM((tp, D_MODEL), F32)],
        compiler_params=_cparams(("arbitrary", "arbitrary")), name="peer_back",
    )(vg, part, ones, gate, h2, g_final.reshape(1, D_MODEL))


def _peer_back(ug, vg, xn, gate, h2, g_final, *, tp, slots_per_step):
    n = xn.shape[0]
    n_sg = PEER_SLOTS // slots_per_step
    rows = pl.BlockSpec((1, slots_per_step, tp, PACKED_WIDTH), lambda i, s: (i, s, 0, 0))
    tok = lambda w: pl.BlockSpec((tp, w), lambda i, s: (i, 0))
    return pl.pallas_call(
        functools.partial(_peer_back_body, slots_per_step=slots_per_step),
        grid=(n // tp, n_sg),
        in_specs=[rows, rows, tok(D_MODEL), tok(PEER_SLOTS), tok(D_MODEL), _full((1, D_MODEL))],
        out_specs=tok(D_MODEL),
        out_shape=jax.ShapeDtypeStruct((n, D_MODEL), F32),
        scratch_shapes=[pltpu.VMEM((tp, D_MODEL), F32), pltpu.VMEM((tp, D_MODEL), F32)],
        compiler_params=_cparams(("arbitrary", "arbitrary")), name="peer_back",
    )(ug, vg, xn, gate, h2, g_final.reshape(1, D_MODEL))


PROJ_TILE = 256
ATTN_TILE = 512
PEER_FRONT_TILE = 256
PEER_SLOT_STEP = 32
SAMPLE_PAGES_PER_STEP = 16


def _retrieve_start(x2, attn, cbz, sga, sgb, mk, mv, n_batch, W, xa_tile):
    n = x2.shape[0]
    h1, qx = _merge(x2, attn, cbz, sga, sgb, W["wfo"], W["wco"], W["wo"], W["g_xattn"], W["wxq"], tm=PROJ_TILE)
    o = _cross_attn(qx.reshape(n_batch, n // n_batch, D_MODEL), mk, mv, tm=xa_tile).reshape(n, D_MODEL)
    h2, xn, eidx, etok, gate = _peer_front(h1, o, W["wxo"], W["g_peer"], W["wpq"], W["ka"], W["kb"],
                                           tm=PEER_FRONT_TILE)
    vg, part = _sc_peer(W["pu"], W["pv"], eidx.reshape(-1), etok.reshape(-1), xn)
    return vg, part, gate, h2


def _retrieve_finish(pending, W):
    vg, part, gate, h2 = pending
    shape4 = (gate.shape[0] // PEER_TILE, PEER_SLOTS, PEER_TILE, PACKED_WIDTH)
    return _peer_back2(vg.reshape(shape4), part, gate, h2, W["g_final"], tp=PEER_TILE, slots_per_step=PEER_SLOT_STEP)


def kernel(x_prompt, x_sample, cache_fox_k, cache_fox_v, cache_fox_logf, cache_mem_k, cache_mem_v, state_conv,
           page_table, mem_prompt, g_mix, w_in, b_forget, w_conv, w_fox_out, w_conv_out, w_o, g_xattn, g_mem,
           w_xq, w_xk, w_xv, w_xo, g_peer, w_peer_q, peer_keys_a, peer_keys_b, peer_u, peer_v, g_final):
    depth = g_mix.shape[0]
    assert depth == 1
    l = 0
    b, t, _ = x_prompt.shape
    bd, s, _ = x_sample.shape
    n_pool = cache_fox_k.shape[1]
    mem_len = mem_prompt.shape[1]

    in_w = _prep_in_weights(w_in[l])
    ka, kb = _prep_peer_keys(peer_keys_a[l], peer_keys_b[l])
    W = dict(wfo=w_fox_out[l].astype(BF16), wco=w_conv_out[l].astype(BF16), wo=w_o[l].astype(BF16),
             g_xattn=g_xattn[l], wxq=w_xq[l].astype(BF16), wxo=w_xo[l].astype(BF16), g_peer=g_peer[l],
             wpq=w_peer_q[l].astype(BF16), ka=ka, kb=kb, pu=_pack_table(peer_u[l]), pv=_pack_table(peer_v[l]),
             g_final=g_final)

    mk, mv = _memory_kv(mem_prompt.reshape(b * mem_len, D_MODEL), g_mem[l], w_xk[l].astype(BF16),
                        w_xv[l].astype(BF16), tm=PROJ_TILE)
    mk3 = mk.reshape(b, mem_len, D_MODEL)
    mv3 = mv.reshape(b, mem_len, D_MODEL)
    prev0 = jnp.zeros((SUBLANES, CONV_WIDTH), F32)
    kf_rows, vf_rows, logf_rows, utail_rows, y_rows = [], [], [], [], []
    pending = None
    for r in range(b):
        xr = x_prompt[r]
        if r >= 2:
            xr, y_rows[r - 2] = lax.optimization_barrier((xr, y_rows[r - 2]))
        q, kf, vf, kb16, vb16, logf, cbz, sga, sgb, utail = _in_proj(
            xr, g_mix[l], *in_w, b_forget[l], w_conv[l], prev0, n_seq=1, seq_len=t, tm=PROJ_TILE)
        frow = _cumsum_time(logf.T[None])
        attn = _fox_prompt(q[None], kb16[None], vb16[None], frow.transpose(0, 2, 1), frow, tq=ATTN_TILE)[0]
        started = _retrieve_start(xr, attn, cbz, sga, sgb, mk3[r:r + 1], mv3[r:r + 1], 1, W, ATTN_TILE)
        if pending is not None:
            y_rows.append(_retrieve_finish(pending, W))
        pending = started
        kf_rows.append(kf)
        vf_rows.append(vf)
        logf_rows.append(logf)
        utail_rows.append(utail)
    heads = lambda a, nb, nt: a.reshape(1, nb, nt, FOX_HEADS, FOX_HEAD_DIM)
    conv_state_p = jnp.stack(utail_rows)[:, SUBLANES - (CONV_K - 1):][None]

    xs = x_sample.reshape(bd * s, D_MODEL)
    if b >= 2:
        xs, y_rows[b - 2] = lax.optimization_barrier((xs, y_rows[b - 2]))
    prev_s =jnp.concatenate([jnp.zeros((bd, SUBLANES - (CONV_K - 1), CONV_WIDTH), F32), state_conv[l]],
                             axis=1).reshape(bd * SUBLANES, CONV_WIDTH)
    qs, kfs, vfs, kbs, vbs, logfs, cbzs, sgas, sgbs, utails = _in_proj(
        xs, g_mix[l], *in_w, b_forget[l], w_conv[l], prev_s, n_seq=bd, seq_len=s, tm=bd * s)
    assert s * FOX_HEADS == SAMPLE_ROWS
    pad_page = lambda a: jnp.pad(a.reshape(bd, s, FOX_WIDTH).transpose(0, 2, 1),
                                 ((0, 0), (0, 0), (0, PAGE_SIZE - s)))
    lf3 = logfs.reshape(bd, s, FOX_HEADS)
    lfcol = lf3.reshape(bd, SAMPLE_ROWS, 1)
    lfrow = jnp.pad(lf3.transpose(0, 2, 1), ((0, 0), (0, 0), (0, LANES - s)))
    page_view = lambda c: c.transpose(0, 2, 3, 1)
    attn_s = _fox_sample(
        page_table, qs.reshape(bd, s, FOX_WIDTH), pad_page(kbs), pad_page(vbs), lfcol, lfrow,
        page_view(cache_fox_k[l]), page_view(cache_fox_v[l]), cache_fox_logf[l].transpose(0, 2, 1),
        pages_per_step=SAMPLE_PAGES_PER_STEP).reshape(bd * s, FOX_WIDTH)
    cmk = cache_mem_k[l].reshape(bd, mem_len, D_MODEL)
    cmv = cache_mem_v[l].reshape(bd, mem_len, D_MODEL)
    started_s = _retrieve_start(xs, attn_s, cbzs, sgas, sgbs, cmk, cmv, bd, W, s)
    y_rows.append(_retrieve_finish(pending, W))
    y_prompt = jnp.stack(y_rows)
    y_sample = _retrieve_finish(started_s, W).reshape(bd, s, D_MODEL)
    conv_state_s = utails.reshape(bd, SUBLANES, CONV_WIDTH)[:, SUBLANES - (CONV_K - 1):][None]

    return (y_prompt, y_sample,
            heads(jnp.stack(kf_rows), b, t), heads(jnp.stack(vf_rows), b, t), jnp.stack(logf_rows)[None],
            mk.reshape(1, b, mem_len, XA_HEADS, XA_HEAD_DIM), mv.reshape(1, b, mem_len, XA_HEADS, XA_HEAD_DIM),
            conv_state_p,
            heads(kfs, bd, s), heads(vfs, bd, s), logfs.reshape(1, bd, s, FOX_HEADS), conv_state_s)
```
